```python
import math
import jax, jax.numpy as jnp
from jax import lax
import numpy as np

D_MODEL = 2048
BATCH = 8
SEQ = 2048
DEPTH = 1
DEC_BATCH = 8
DEC_SEQ = 64
PAST_LEN = 1024

CHUNK = 64
D_INNER = 2 * D_MODEL
SSD_HEADDIM = 64
SSD_HEADS = D_INNER // SSD_HEADDIM
SSD_GROUPS = 8
D_STATE = 128
CONV_W = 4
CONV_DIM = D_INNER + 2 * SSD_GROUPS * D_STATE
SB_HEADDIM = 128
SB_HEADS = D_MODEL // SB_HEADDIM
D_ATTN = SB_HEADS * SB_HEADDIM
SB_QBLOCK = 128
SB_SCALE = 1.0 / math.sqrt(SB_HEADDIM)
D_FF = 4 * D_MODEL
ALPHA = (2.0 * DEPTH) ** 0.25
BETA = (8.0 * DEPTH) ** -0.25
LN_EPS = 1e-5
RMS_EPS = 1e-5
SPLITS = list(np.cumsum([D_INNER, CONV_DIM, SSD_HEADS, D_ATTN, D_ATTN, D_ATTN]))
D_IN_PROJ = D_INNER + CONV_DIM + SSD_HEADS + 3 * D_ATTN + 2 * D_MODEL

kernel_name = "ssd_stickbreak_gated_deepnorm_stream"


def layer_norm(x, g, b):
    xf = x.astype(jnp.float32)
    mu = jnp.mean(xf, axis=-1, keepdims=True)
    var = jnp.mean(jnp.square(xf - mu), axis=-1, keepdims=True)
    return ((xf - mu) * lax.rsqrt(var + LN_EPS) * g + b).astype(x.dtype)


def gated_rmsnorm(y, z, w):
    sh = y.shape
    yg = (y * jax.nn.silu(z)).astype(jnp.float32).reshape(sh[:-1] + (SSD_GROUPS, D_INNER // SSD_GROUPS))
    yg = yg * lax.rsqrt(jnp.mean(jnp.square(yg), axis=-1, keepdims=True) + RMS_EPS)
    return (yg.reshape(sh) * w).astype(y.dtype)


def causal_conv(xbc, prev, w, b):
    T = xbc.shape[1]
    xp = jnp.concatenate([prev.astype(xbc.dtype), xbc], axis=1)
    y = b + sum(xp[:, j:j + T] * w[j] for j in range(CONV_W))
    return jax.nn.silu(y), xp[:, -(CONV_W - 1):]


def segsum(a):
    T = a.shape[-1]
    ar = jnp.broadcast_to(a[..., :, None], a.shape + (T,))
    ar = jnp.where(jnp.tril(jnp.ones((T, T), bool), -1), ar, 0.0)
    cs = jnp.cumsum(ar, axis=-2)
    return jnp.where(jnp.tril(jnp.ones((T, T), bool)), cs, -jnp.inf)


def ssd_scan(xh, dt, a, bm, cm, h0):
    b, T, H, P = xh.shape
    G, N = bm.shape[2], bm.shape[3]
    R = H // G
    lc = min(CHUNK, T)
    nc = T // lc
    xd = (xh * dt[..., None]).reshape(b, nc, lc, G, R, P)
    dA = (dt * a).reshape(b, nc, lc, G, R).transpose(0, 3, 4, 1, 2)
    Bc = bm.reshape(b, nc, lc, G, N)
    Cc = cm.reshape(b, nc, lc, G, N)
    a_cs = jnp.cumsum(dA, axis=-1)
    Lm = jnp.exp(segsum(dA))
    y_diag = jnp.einsum('bclgn,bcsgn,bgrcls,bcsgrp->bclgrp', Cc, Bc, Lm, xd)
    decay_states = jnp.exp(a_cs[..., -1:] - a_cs)
    states = jnp.einsum('bclgn,bgrcl,bclgrp->bcgrpn', Bc, decay_states, xd)
    h0r = h0.reshape(b, G, R, P, N).astype(states.dtype)
    states = jnp.concatenate([h0r[:, None], states], axis=1)
    chunk_tot = jnp.pad(a_cs[..., -1], ((0, 0), (0, 0), (0, 0), (1, 0)))
    decay_chunk = jnp.exp(segsum(chunk_tot))
    new_states = jnp.einsum('bgrzc,bcgrpn->bzgrpn', decay_chunk, states)
    states_in, h_last = new_states[:, :-1], new_states[:, -1]
    y_off = jnp.einsum('bclgn,bcgrpn,bgrcl->bclgrp', Cc, states_in, jnp.exp(a_cs))
    y = (y_diag + y_off).reshape(b, T, H, P)
    return y.astype(xh.dtype), h_last.reshape(b, H, P, N).astype(h0.dtype)


def sb_block(qb, k, v, pos0):
    Q, Tk = qb.shape[1], k.shape[1]
    z = jnp.einsum('bqhd,bkhd->bhqk', qb, k).astype(jnp.float32) * SB_SCALE
    mask = jnp.arange(Tk)[None, :] < (pos0 + jnp.arange(Q))[:, None]
    log_1m = jnp.where(mask, jax.nn.log_sigmoid(-z), 0.0)
    suffix = lax.cumsum(log_1m, axis=3, reverse=True) - log_1m
    w = jnp.where(mask, jnp.exp(jax.nn.log_sigmoid(z) + suffix), 0.0)
    return jnp.einsum('bhqk,bkhd->bqhd', w.astype(v.dtype), v)


def sb_attention(q, k, v):
    b, T, H, Dh = q.shape
    off = k.shape[1] - T
    if T > SB_QBLOCK and T % SB_QBLOCK == 0:
        nb = T // SB_QBLOCK
        qb = q.reshape(b, nb, SB_QBLOCK, H, Dh).transpose(1, 0, 2, 3, 4)
        starts = off + jnp.arange(nb) * SB_QBLOCK
        out = lax.map(lambda a: sb_block(a[0], k, v, a[1]), (qb, starts))
        return out.transpose(1, 0, 2, 3, 4).reshape(b, T, H, Dh)
    return sb_block(q, k, v, off)


def trunk_layer(x, conv_prev, h0, k_past, v_past, w_in, b_gate, conv_w, conv_b, dt_bias, a_log,
                d_skip, ssm_norm_w, w_br_ssd, w_br_attn, w_out, ln1_g, ln1_b, w_up, b_up,
                w_down, b_down, ln2_g, ln2_b):
    b, T, _ = x.shape
    proj = jnp.einsum('btd,de->bte', x, w_in)
    z, xbc, dt_raw, q, k, v, gate_logits = jnp.split(proj, SPLITS, axis=-1)
    xbc, conv_new = causal_conv(xbc, conv_prev, conv_w, conv_b)
    xs, bm, cm = jnp.split(xbc, [D_INNER, D_INNER + SSD_GROUPS * D_STATE], axis=-1)
    xh = xs.reshape(b, T, SSD_HEADS, SSD_HEADDIM)
    bm = bm.reshape(b, T, SSD_GROUPS, D_STATE)
    cm = cm.reshape(b, T, SSD_GROUPS, D_STATE)
    dt = jax.nn.softplus(dt_raw.astype(jnp.float32) + dt_bias)
    a = -jnp.exp(a_log.astype(jnp.float32))
    y_ssd, h_new = ssd_scan(xh, dt, a, bm, cm, h0)
    y_ssd = (y_ssd + d_skip[:, None] * xh).reshape(b, T, D_INNER)
    y_ssd = gated_rmsnorm(y_ssd, z, ssm_norm_w)
    q = q.reshape(b, T, SB_HEADS, SB_HEADDIM)
    k = k.reshape(b, T, SB_HEADS, SB_HEADDIM)
    v = v.reshape(b, T, SB_HEADS, SB_HEADDIM)
    k_all = jnp.concatenate([k_past.astype(k.dtype), k], axis=1)
    v_all = jnp.concatenate([v_past.astype(v.dtype), v], axis=1)
    y_sb = sb_attention(q, k_all, v_all).reshape(b, T, D_ATTN)
    g_ssd, g_sb = jnp.split(jax.nn.sigmoid(gate_logits + b_gate), 2, axis=-1)
    merged = g_ssd * (y_ssd @ w_br_ssd) + g_sb * (y_sb @ w_br_attn)
    x1 = layer_norm(ALPHA * x + merged @ w_out, ln1_g, ln1_b)
    hdn = jnp.square(jax.nn.relu(x1 @ w_up + b_up))
    x2 = layer_norm(ALPHA * x1 + hdn @ w_down + b_down, ln2_g, ln2_b)
    return x2, conv_new, h_new, k, v


def setup_inputs(seed: int = 0) -> dict:
    key = jax.random.key(seed)
    ks = jax.random.split(key, 26)

    def nrm(k, shape, scale):
        return jax.random.normal(k, shape, jnp.float32) * scale

    dt0 = jnp.exp(jax.random.uniform(ks[9], (DEPTH, SSD_HEADS), jnp.float32, math.log(1e-3), math.log(1e-1)))
    return {
        "x_prompt": nrm(ks[0], (BATCH, SEQ, D_MODEL), 1.0),
        "x_sample": nrm(ks[1], (DEC_BATCH, DEC_SEQ, D_MODEL), 1.0),
        "cache_conv": nrm(ks[2], (DEPTH, DEC_BATCH, CONV_W - 1, CONV_DIM), 1.0),
        "state_ssm": nrm(ks[3], (DEPTH, DEC_BATCH, SSD_HEADS, SSD_HEADDIM, D_STATE), 0.1),
        "cache_k": nrm(ks[4], (DEPTH, DEC_BATCH, PAST_LEN, SB_HEADS, SB_HEADDIM), 1.0),
        "cache_v": nrm(ks[5], (DEPTH, DEC_BATCH, PAST_LEN, SB_HEADS, SB_HEADDIM), 1.0),
        "w_in": nrm(ks[6], (DEPTH, D_MODEL, D_IN_PROJ), D_MODEL ** -0.5),
        "b_gate": nrm(ks[7], (DEPTH, 2 * D_MODEL), 0.02),
        "conv_w": nrm(ks[8], (DEPTH, CONV_W, CONV_DIM), CONV_W ** -0.5),
        "conv_b": nrm(ks[10], (DEPTH, CONV_DIM), 0.02),
        "dt_bias": dt0 + jnp.log(-jnp.expm1(-dt0)),
        "a_log": jnp.log(jax.random.uniform(ks[11], (DEPTH, SSD_HEADS), jnp.float32, 1.0, 16.0)),
        "d_skip": 1.0 + nrm(ks[12], (DEPTH, SSD_HEADS), 0.1),
        "ssm_norm_w": 1.0 + nrm(ks[13], (DEPTH, D_INNER), 0.02),
        "w_br_ssd": nrm(ks[14], (DEPTH, D_INNER, D_MODEL), D_INNER ** -0.5),
        "w_br_attn": nrm(ks[15], (DEPTH, D_ATTN, D_MODEL), D_ATTN ** -0.5),
        "w_out": nrm(ks[16], (DEPTH, D_MODEL, D_MODEL), BETA * D_MODEL ** -0.5),
        "ln1_g": 1.0 + nrm(ks[17], (DEPTH, D_MODEL), 0.02),
        "ln1_b": nrm(ks[18], (DEPTH, D_MODEL), 0.02),
        "w_up": nrm(ks[19], (DEPTH, D_MODEL, D_FF), D_MODEL ** -0.5),
        "b_up": nrm(ks[20], (DEPTH, D_FF), 0.02),
        "w_down": nrm(ks[21], (DEPTH, D_FF, D_MODEL), BETA * D_FF ** -0.5),
        "b_down": nrm(ks[22], (DEPTH, D_MODEL), 0.02),
        "ln2_g": 1.0 + nrm(ks[23], (DEPTH, D_MODEL), 0.02),
        "ln2_b": nrm(ks[24], (DEPTH, D_MODEL), 0.02),
    }


def reference(x_prompt, x_sample, cache_conv, state_ssm, cache_k, cache_v, w_in, b_gate, conv_w,
              conv_b, dt_bias, a_log, d_skip, ssm_norm_w, w_br_ssd, w_br_attn, w_out, ln1_g, ln1_b,
              w_up, b_up, w_down, b_down, ln2_g, ln2_b):
    hp, hs = x_prompt, x_sample
    conv_p, ssm_p, k_p, v_p = [], [], [], []
    conv_s, ssm_s, k_s, v_s = [], [], [], []
    for l in range(DEPTH):
        wl = (w_in[l], b_gate[l], conv_w[l], conv_b[l], dt_bias[l], a_log[l], d_skip[l],
              ssm_norm_w[l], w_br_ssd[l], w_br_attn[l], w_out[l], ln1_g[l], ln1_b[l],
              w_up[l], b_up[l], w_down[l], b_down[l], ln2_g[l], ln2_b[l])
        zc = jnp.zeros((BATCH, CONV_W - 1, CONV_DIM), hp.dtype)
        zh = jnp.zeros((BATCH, SSD_HEADS, SSD_HEADDIM, D_STATE), hp.dtype)
        zk = jnp.zeros((BATCH, 0, SB_HEADS, SB_HEADDIM), hp.dtype)
        hp, c1, s1, k1, v1 = trunk_layer(hp, zc, zh, zk, zk, *wl)
        conv_p.append(c1); ssm_p.append(s1); k_p.append(k1); v_p.append(v1)
        hs, c2, s2, k2, v2 = trunk_layer(hs, cache_conv[l], state_ssm[l], cache_k[l], cache_v[l], *wl)
        conv_s.append(c2); ssm_s.append(s2); k_s.append(k2); v_s.append(v2)
    return (hp, hs, jnp.stack(conv_p), jnp.stack(ssm_p), jnp.stack(k_p), jnp.stack(v_p),
            jnp.stack(conv_s), jnp.stack(ssm_s), jnp.stack(k_s), jnp.stack(v_s))
```

```python
import functools
import math

import jax
import jax.numpy as jnp
from jax import lax
from jax.experimental import pallas as pl
from jax.experimental.pallas import tpu as pltpu

D_MODEL = 2048
D_INNER = 2 * D_MODEL
SSD_HEADDIM = 64
SSD_HEADS = D_INNER // SSD_HEADDIM
SSD_GROUPS = 8
HEADS_PER_GROUP = SSD_HEADS // SSD_GROUPS
GROUP_W = D_INNER // SSD_GROUPS
D_STATE = 128
CONV_W = 4
CONV_DIM = D_INNER + 2 * SSD_GROUPS * D_STATE
SB_HEADDIM = 128
SB_HEADS = D_MODEL // SB_HEADDIM
D_ATTN = SB_HEADS * SB_HEADDIM
SB_SCALE = 1.0 / math.sqrt(SB_HEADDIM)
D_FF = 4 * D_MODEL
DEPTH = 1
ALPHA = (2.0 * DEPTH) ** 0.25
LN_EPS = 1e-5
RMS_EPS = 1e-5
CHUNK = 64

V7X_VMEM_BYTES = 64 * 1024 * 1024
VMEM_LIMIT = 56 * 1024 * 1024

F32 = jnp.float32
BF16 = jnp.bfloat16


def _cparams(sem):
    return pltpu.CompilerParams(dimension_semantics=sem, vmem_limit_bytes=VMEM_LIMIT)


def _sigmoid(x):
    return 1.0 / (1.0 + jnp.exp(-x))


def _softplus(x):
    return jnp.maximum(x, 0.0) + jnp.log1p(jnp.exp(-jnp.abs(x)))


def _split_bf16(x):
    hi = x.astype(BF16)
    lo = (x - hi.astype(F32)).astype(BF16)
    return hi, lo


def _layer_norm(x, g, b):
    mu = jnp.mean(x, axis=-1, keepdims=True)
    xc = x - mu
    var = jnp.mean(xc * xc, axis=-1, keepdims=True)
    return xc * lax.rsqrt(var + LN_EPS) * g + b


def _inproj_kernel(x_ref, w_ref, wdt_ref, bg_ref, z_ref, xbc_ref, q_ref, k_ref, v_ref, g_ref, dt_ref, *, ends):
    j = pl.program_id(1)
    ez, exbc, eq, ek, ev, _ = ends

    def proj():
        return jnp.dot(x_ref[...], w_ref[...], preferred_element_type=F32)

    @pl.when(j == 0)
    def _():
        dt_ref[...] = jnp.dot(x_ref[...], wdt_ref[...], preferred_element_type=F32)

    @pl.when(j < ez)
    def _():
        z_ref[...] = proj()

    @pl.when((j >= ez) & (j < exbc))
    def _():
        xbc_ref[...] = proj()

    @pl.when((j >= exbc) & (j < eq))
    def _():
        q_ref[...] = proj().astype(BF16)

    @pl.when((j >= eq) & (j < ek))
    def _():
        k_ref[...] = proj()

    @pl.when((j >= ek) & (j < ev))
    def _():
        v_ref[...] = proj()

    @pl.when(j >= ev)
    def _():
        g_ref[...] = _sigmoid(proj() + bg_ref[...])


def _in_proj(xb, w_main, w_dt, b_gate, tm, tn):
    m = xb.shape[0]
    widths = (D_INNER, CONV_DIM, D_ATTN, D_ATTN, D_ATTN, 2 * D_MODEL)
    ends, e = [], 0
    for w in widths:
        assert w % tn == 0
        e += w // tn
        ends.append(e)
    starts = [0] + ends[:-1]
    nblk = [w // tn for w in widths]

    def seg_map(s, n):
        return lambda i, j: (i, jnp.clip(j - s, 0, n - 1))

    out_shapes = (
        jax.ShapeDtypeStruct((m, D_INNER), F32),
        jax.ShapeDtypeStruct((m, CONV_DIM), F32),
        jax.ShapeDtypeStruct((m, D_ATTN), BF16),
        jax.ShapeDtypeStruct((m, D_ATTN), F32),
        jax.ShapeDtypeStruct((m, D_ATTN), F32),
        jax.ShapeDtypeStruct((m, 2 * D_MODEL), F32),
        jax.ShapeDtypeStruct((m, 128), F32),
    )
    out_specs = [pl.BlockSpec((tm, tn), seg_map(starts[s], nblk[s])) for s in range(6)]
    out_specs.append(pl.BlockSpec((tm, 128), lambda i, j: (i, 0)))
    return pl.pallas_call(
        functools.partial(_inproj_kernel, ends=tuple(ends)),
        grid=(m // tm, ends[-1]),
        in_specs=[
            pl.BlockSpec((tm, D_MODEL), lambda i, j: (i, 0)),
            pl.BlockSpec((D_MODEL, tn), lambda i, j: (0, j)),
            pl.BlockSpec((D_MODEL, 128), lambda i, j: (0, 0)),
            pl.BlockSpec((1, tn), lambda i, j: (0, jnp.clip(j - starts[5], 0, nblk[5] - 1))),
        ],
        out_specs=out_specs,
        out_shape=out_shapes,
        compiler_params=_cparams(("parallel", "arbitrary")),
        name="in_proj",
    )(xb, w_main, w_dt, b_gate)


def _ssd_kernel(*refs, has_history):
    if has_history:
        (xbc_ref, dtr_ref, z_ref, cprev_ref, h0_ref, convw_ref, convb_ref, dtb_ref, aexp_ref, dskip_ref,
         nw_ref, e_ref, tri_ref, y_ref, ht_ref, xpad, state) = refs
    else:
        (xbc_ref, dtr_ref, z_ref, convw_ref, convb_ref, dtb_ref, aexp_ref, dskip_ref,
         nw_ref, e_ref, tri_ref, y_ref, ht_ref, xpad, state) = refs
    c = pl.program_id(1)
    nc = pl.num_programs(1)
    L = CHUNK

    @pl.when(c == 0)
    def _():
        if has_history:
            xpad[0:8, :] = cprev_ref[0]
            state[...] = h0_ref[0]
        else:
            xpad[0:8, :] = jnp.zeros((8, CONV_DIM), F32)
            state[...] = jnp.zeros(state.shape, F32)

    xpad[8:8 + L, :] = xbc_ref[0]

    dt = _softplus(dtr_ref[0][:, :SSD_HEADS] + dtb_ref[...])
    dt_hi, dt_lo = _split_bf16(dt)
    tri = tri_ref[...]
    row = lax.broadcasted_iota(jnp.int32, (L, GROUP_W), 0)
    lane = lax.broadcasted_iota(jnp.int32, (L, GROUP_W), 1)
    pos_in_head = lane & (SSD_HEADDIM - 1)
    diag_sel = row == pos_in_head
    causal = row >= pos_in_head
    lane128 = lax.broadcasted_iota(jnp.int32, (L, 128), 1)
    lo_half = lane128 < SSD_HEADDIM

    def conv_cols(start, width):
        sl = pl.ds(start, width)
        acc = convb_ref[:, sl] + convw_ref[CONV_W - 1:CONV_W, sl] * xpad[8:8 + L, sl]
        for t in range(CONV_W - 1):
            acc = acc + convw_ref[t:t + 1, sl] * xpad[5 + t:5 + t + L, sl]
        return acc * _sigmoid(acc)

    def group_body(g, carry):
        xoff = pl.multiple_of(g * GROUP_W, GROUP_W)
        boff = pl.multiple_of(D_INNER + g * D_STATE, D_STATE)
        coff = pl.multiple_of(D_INNER + SSD_GROUPS * D_STATE + g * D_STATE, D_STATE)
        xs = conv_cols(xoff, GROUP_W)
        bg = conv_cols(boff, D_STATE).astype(BF16)
        cg = conv_cols(coff, D_STATE).astype(BF16)

        eg = e_ref[:, pl.ds(xoff, GROUP_W)]
        dt_exp = (jnp.dot(dt_hi, eg, preferred_element_type=F32)
                  + jnp.dot(dt_lo, eg, preferred_element_type=F32))
        da = dt_exp * aexp_ref[:, pl.ds(xoff, GROUP_W)]
        acs = jnp.dot(tri, da, preferred_element_type=F32, precision=lax.Precision.HIGHEST)
        a_last = acs[L - 1:L, :]
        eacs = jnp.exp(acs)
        xd = xs * dt_exp
        xdd = (xd * jnp.exp(a_last - acs)).astype(BF16)
        xdb = xd.astype(BF16)

        b2 = jnp.concatenate([bg, bg], axis=0)
        cb2 = lax.dot_general(cg, b2, (((1,), (1,)), ((), ())), preferred_element_type=F32)
        rowform = jnp.sum(jnp.where(diag_sel, acs, 0.0), axis=0, keepdims=True)
        lmat = jnp.where(causal, jnp.exp(jnp.minimum(acs - rowform, 0.0)), 0.0)
        mmat = (lmat * jnp.concatenate([cb2] * (GROUP_W // 128), axis=1)).astype(BF16)
        zero = jnp.zeros((L, 128), BF16)
        ydiag = []
        for p in range(GROUP_W // 128):
            xp = xdb[:, 128 * p:128 * (p + 1)]
            bd = jnp.concatenate([jnp.where(lo_half, xp, zero), jnp.where(lo_half, zero, xp)], axis=0)
            ydiag.append(jnp.dot(mmat[:, 128 * p:128 * (p + 1)], bd, preferred_element_type=F32))
        ydiag = jnp.concatenate(ydiag, axis=1)

        st = state[g]
        yoff = jnp.dot(cg, st.astype(BF16), preferred_element_type=F32) * eacs
        y = ydiag + yoff + dskip_ref[:, pl.ds(xoff, GROUP_W)] * xs

        snew = lax.dot_general(bg, xdd, (((0,), (0,)), ((), ())), preferred_element_type=F32)
        state[g] = st * jnp.exp(a_last) + snew

        zg = z_ref[0, :, pl.ds(xoff, GROUP_W)]
        yg = y * (zg * _sigmoid(zg))
        ms = jnp.mean(yg * yg, axis=-1, keepdims=True)
        y_ref[0, :, pl.ds(xoff, GROUP_W)] = (yg * lax.rsqrt(ms + RMS_EPS) * nw_ref[:, pl.ds(xoff, GROUP_W)]).astype(y_ref.dtype)
        return carry

    lax.fori_loop(0, SSD_GROUPS, group_body, 0)
    xpad[0:8, :] = xpad[L:L + 8, :]

    @pl.when(c == nc - 1)
    def _():
        ht_ref[0] = state[...]


def _ssd(xbc, dtr, z, conv_prev8, h0t, conv_w, conv_b, dt_bias, a_exp, dskip_exp, norm_w, e_mat, tri, b, t):
    nc = t // CHUNK
    has_history = conv_prev8 is not None
    tok = lambda bi, c: (bi, c, 0)
    const2 = lambda bi, c: (0, 0)
    in_specs = [
        pl.BlockSpec((1, CHUNK, CONV_DIM), tok),
        pl.BlockSpec((1, CHUNK, 128), tok),
        pl.BlockSpec((1, CHUNK, D_INNER), tok),
    ]
    args = [xbc.reshape(b, t, CONV_DIM), dtr.reshape(b, t, 128), z.reshape(b, t, D_INNER)]
    if has_history:
        in_specs += [
            pl.BlockSpec((1, 8, CONV_DIM), lambda bi, c: (bi, 0, 0)),
            pl.BlockSpec((1, SSD_GROUPS, D_STATE, GROUP_W), lambda bi, c: (bi, 0, 0, 0)),
        ]
        args += [conv_prev8, h0t]
    in_specs += [
        pl.BlockSpec((CONV_W, CONV_DIM), const2),
        pl.BlockSpec((1, CONV_DIM), const2),
        pl.BlockSpec((1, SSD_HEADS), const2),
        pl.BlockSpec((1, D_INNER), const2),
        pl.BlockSpec((1, D_INNER), const2),
        pl.BlockSpec((1, D_INNER), const2),
        pl.BlockSpec((SSD_HEADS, D_INNER), const2),
        pl.BlockSpec((CHUNK, CHUNK), const2),
    ]
    args += [conv_w, conv_b, dt_bias, a_exp, dskip_exp, norm_w, e_mat, tri]
    return pl.pallas_call(
        functools.partial(_ssd_kernel, has_history=has_history),
        grid=(b, nc),
        in_specs=in_specs,
        out_specs=[
            pl.BlockSpec((1, CHUNK, D_INNER), tok),
            pl.BlockSpec((1, SSD_GROUPS, D_STATE, GROUP_W), lambda bi, c: (bi, 0, 0, 0)),
        ],
        out_shape=(
            jax.ShapeDtypeStruct((b, t, D_INNER), BF16),
            jax.ShapeDtypeStruct((b, SSD_GROUPS, D_STATE, GROUP_W), F32),
        ),
        scratch_shapes=[
            pltpu.VMEM((8 + CHUNK, CONV_DIM), F32),
            pltpu.VMEM((SSD_GROUPS, D_STATE, GROUP_W), F32),
        ],
        compiler_params=_cparams(("parallel", "arbitrary")),
        name="conv_ssd",
    )(*args)


def _sb_tile(q, kb, vb, u, r, acc, masked):
    z = lax.dot_general(q, kb, (((1,), (1,)), ((), ())), preferred_element_type=F32) * SB_SCALE
    sp = _softplus(z)
    lm = -sp
    ls = z - sp
    if masked:
        tq, tk = z.shape
        visible = lax.broadcasted_iota(jnp.int32, (tq, tk), 1) < lax.broadcasted_iota(jnp.int32, (tq, tk), 0)
        lm = jnp.where(visible, lm, 0.0)
    hi, lo = _split_bf16(lm)
    suffix = jnp.dot(hi, u, preferred_element_type=F32) + jnp.dot(lo, u, preferred_element_type=F32)
    w = jnp.exp(ls + suffix + r)
    if masked:
        w = jnp.where(visible, w, 0.0)
    acc = acc + jnp.dot(w.astype(BF16), vb, preferred_element_type=F32)
    r = r + jnp.sum(lm, axis=-1, keepdims=True)
    return r, acc


def _attn_prompt_kernel(q_ref, k_ref, v_ref, u_ref, o_ref, *, tq):
    qi = pl.program_id(2)
    q = q_ref[0]
    u = u_ref[...]

    def kv(jb):
        off = pl.multiple_of(jb * tq, tq)
        return k_ref[0, pl.ds(off, tq), :].astype(BF16), v_ref[0, pl.ds(off, tq), :].astype(BF16)

    kb, vb = kv(qi)
    r, acc = _sb_tile(q, kb, vb, u, jnp.zeros((tq, 1), F32), jnp.zeros((tq, SB_HEADDIM), F32), True)

    def body(t, carry):
        r, acc = carry
        kb, vb = kv(qi - 1 - t)
        return _sb_tile(q, kb, vb, u, r, acc, False)

    r, acc = lax.fori_loop(0, qi, body, (r, acc))
    o_ref[0] = acc.astype(o_ref.dtype)


def _attn_prompt(q, k, v, u, b, t, tq):
    q3, k3, v3 = q.reshape(b, t, D_ATTN), k.reshape(b, t, D_ATTN), v.reshape(b, t, D_ATTN)
    return pl.pallas_call(
        functools.partial(_attn_prompt_kernel, tq=tq),
        grid=(b, SB_HEADS, t // tq),
        in_specs=[
            pl.BlockSpec((1, tq, SB_HEADDIM), lambda bi, h, qi: (bi, qi, h)),
            pl.BlockSpec((1, t, SB_HEADDIM), lambda bi, h, qi: (bi, 0, h)),
            pl.BlockSpec((1, t, SB_HEADDIM), lambda bi, h, qi: (bi, 0, h)),
            pl.BlockSpec((tq, tq), lambda bi, h, qi: (0, 0)),
        ],
        out_specs=pl.BlockSpec((1, tq, SB_HEADDIM), lambda bi, h, qi: (bi, qi, h)),
        out_shape=jax.ShapeDtypeStruct((b, t, D_ATTN), BF16),
        compiler_params=_cparams(("parallel", "parallel", "arbitrary")),
        name="sb_attn_prompt",
    )(q3, k3, v3, u)


def _attn_sample_kernel(q_ref, kn_ref, vn_ref, kc_ref, vc_ref, un_ref, uc_ref, o_ref, *, tn, tc, n_cache):
    q = q_ref[0]
    r, acc = _sb_tile(q, kn_ref[0].astype(BF16), vn_ref[0].astype(BF16), un_ref[...],
                      jnp.zeros((tn, 1), F32), jnp.zeros((tn, SB_HEADDIM), F32), True)
    uc = uc_ref[...]

    def body(i, carry):
        r, acc = carry
        off = pl.multiple_of((n_cache - 1 - i) * tc, tc)
        kb = kc_ref[0, pl.ds(off, tc), :].astype(BF16)
        vb = vc_ref[0, pl.ds(off, tc), :].astype(BF16)
        return _sb_tile(q, kb, vb, uc, r, acc, False)

    r, acc = lax.fori_loop(0, n_cache, body, (r, acc))
    o_ref[0] = acc.astype(o_ref.dtype)


def _attn_sample(q, k, v, cache_k, cache_v, un, uc, b, t, past, tc):
    q3, k3, v3 = q.reshape(b, t, D_ATTN), k.reshape(b, t, D_ATTN), v.reshape(b, t, D_ATTN)
    ck, cv = cache_k.reshape(b, past, D_ATTN), cache_v.reshape(b, past, D_ATTN)
    new = lambda bi, h: (bi, 0, h)
    return pl.pallas_call(
        functools.partial(_attn_sample_kernel, tn=t, tc=tc, n_cache=past // tc),
        grid=(b, SB_HEADS),
        in_specs=[
            pl.BlockSpec((1, t, SB_HEADDIM), new),
            pl.BlockSpec((1, t, SB_HEADDIM), new),
            pl.BlockSpec((1, t, SB_HEADDIM), new),
            pl.BlockSpec((1, past, SB_HEADDIM), new),
            pl.BlockSpec((1, past, SB_HEADDIM), new),
            pl.BlockSpec((t, t), lambda bi, h: (0, 0)),
            pl.BlockSpec((tc, tc), lambda bi, h: (0, 0)),
        ],
        out_specs=pl.BlockSpec((1, t, SB_HEADDIM), new),
        out_shape=jax.ShapeDtypeStruct((b, t, D_ATTN), BF16),
        compiler_params=_cparams(("parallel", "parallel")),
        name="sb_attn_sample",
    )(q3, k3, v3, ck, cv, un, uc)


def _merge_kernel(ya_ref, yb_ref, wa_ref, wb_ref, ga_ref, gb_ref, o_ref):
    a = jnp.dot(ya_ref[...], wa_ref[...], preferred_element_type=F32)
    bb = jnp.dot(yb_ref[...], wb_ref[...], preferred_element_type=F32)
    o_ref[...] = (ga_ref[...] * a + gb_ref[...] * bb).astype(o_ref.dtype)


def _merge(ya, yb, wa, wb, gate, tm, tn):
    m = ya.shape[0]
    nb = D_MODEL // tn
    return pl.pallas_call(
        _merge_kernel,
        grid=(m // tm, nb),
        in_specs=[
            pl.BlockSpec((tm, D_INNER), lambda i, j: (i, 0)),
            pl.BlockSpec((tm, D_ATTN), lambda i, j: (i, 0)),
            pl.BlockSpec((D_INNER, tn), lambda i, j: (0, j)),
            pl.BlockSpec((D_ATTN, tn), lambda i, j: (0, j)),
            pl.BlockSpec((tm, tn), lambda i, j: (i, j)),
            pl.BlockSpec((tm, tn), lambda i, j: (i, j + nb)),
        ],
        out_specs=pl.BlockSpec((tm, tn), lambda i, j: (i, j)),
        out_shape=jax.ShapeDtypeStruct((m, D_MODEL), BF16),
        compiler_params=_cparams(("parallel", "arbitrary")),
        name="branch_merge",
    )(ya, yb, wa, wb, gate, gate)


def _outproj_kernel(m_ref, w_ref, x_ref, g_ref, b_ref, o_ref):
    mix = jnp.dot(m_ref[...], w_ref[...], preferred_element_type=F32)
    o_ref[...] = _layer_norm(ALPHA * x_ref[...] + mix, g_ref[...], b_ref[...])


def _outproj_ln(merged, w_out, x, g, bb, tm):
    m = x.shape[0]
    row = lambda i: (i, 0)
    const = lambda i: (0, 0)
    return pl.pallas_call(
        _outproj_kernel,
        grid=(m // tm,),
        in_specs=[
            pl.BlockSpec((tm, D_MODEL), row),
            pl.BlockSpec((D_MODEL, D_MODEL), const),
            pl.BlockSpec((tm, D_MODEL), row),
            pl.BlockSpec((1, D_MODEL), const),
            pl.BlockSpec((1, D_MODEL), const),
        ],
        out_specs=pl.BlockSpec((tm, D_MODEL), row),
        out_shape=jax.ShapeDtypeStruct((m, D_MODEL), F32),
        compiler_params=_cparams(("parallel",)),
        name="outproj_ln",
    )(merged, w_out, x, g, bb)


def _mlp_kernel(x_ref, wu_ref, bu_ref, wd_ref, bd_ref, g_ref, b_ref, o_ref, xb, acc):
    f = pl.program_id(1)

    @pl.when(f == 0)
    def _():
        xb[...] = x_ref[...].astype(BF16)
        acc[...] = jnp.zeros(acc.shape, F32)

    h = jnp.dot(xb[...], wu_ref[...], preferred_element_type=F32) + bu_ref[...]
    h = jnp.square(jnp.maximum(h, 0.0)).astype(BF16)
    acc[...] += jnp.dot(h, wd_ref[...], preferred_element_type=F32)

    @pl.when(f == pl.num_programs(1) - 1)
    def _():
        o_ref[...] = _layer_norm(ALPHA * x_ref[...] + acc[...] + bd_ref[...], g_ref[...], b_ref[...])


def _mlp_ln(x1, w_up, b_up, w_down, b_down, g, bb, tm, tf):
    m = x1.shape[0]
    row = lambda i, f: (i, 0)
    const = lambda i, f: (0, 0)
    return pl.pallas_call(
        _mlp_kernel,
        grid=(m // tm, D_FF // tf),
        in_specs=[
            pl.BlockSpec((tm, D_MODEL), row),
            pl.BlockSpec((D_MODEL, tf), lambda i, f: (0, f)),
            pl.BlockSpec((1, tf), lambda i, f: (0, f)),
            pl.BlockSpec((tf, D_MODEL), lambda i, f: (f, 0)),
            pl.BlockSpec((1, D_MODEL), const),
            pl.BlockSpec((1, D_MODEL), const),
            pl.BlockSpec((1, D_MODEL), const),
        ],
        out_specs=pl.BlockSpec((tm, D_MODEL), row),
        out_shape=jax.ShapeDtypeStruct((m, D_MODEL), F32),
        scratch_shapes=[pltpu.VMEM((tm, D_MODEL), BF16), pltpu.VMEM((tm, D_MODEL), F32)],
        compiler_params=_cparams(("parallel", "arbitrary")),
        name="mlp_ln",
    )(x1, w_up, b_up, w_down, b_down, g, bb)


def _trunk(x, conv_prev, h0, k_past, v_past, p):
    b, t, _ = x.shape
    m = b * t
    x2 = x.reshape(m, D_MODEL)
    tm = min(m, 1024)
    z, xbc, q, k, v, gate, dtr = _in_proj(x2.astype(BF16), p["w_main"], p["w_dt"], p["b_gate"], tm, 512)

    if conv_prev is None:
        conv_prev8, h0t = None, None
    else:
        conv_prev8 = jnp.pad(conv_prev, ((0, 0), (8 - (CONV_W - 1), 0), (0, 0)))
        h0t = h0.reshape(b, SSD_GROUPS, GROUP_W, D_STATE).transpose(0, 1, 3, 2)
    y_ssd, ht = _ssd(xbc, dtr, z, conv_prev8, h0t, p["conv_w"], p["conv_b"], p["dt_bias"], p["a_exp"],
                     p["dskip_exp"], p["norm_w"], p["e_mat"], p["tri"], b, t)
    h_new = ht.transpose(0, 1, 3, 2).reshape(b, SSD_HEADS, SSD_HEADDIM, D_STATE)
    conv_new = xbc.reshape(b, t, CONV_DIM)[:, t - (CONV_W - 1):, :]

    if k_past is None:
        tq = 256
        y_sb = _attn_prompt(q, k, v, p["u"][tq], b, t, tq)
    else:
        tc = 256
        y_sb = _attn_sample(q, k, v, k_past, v_past, p["u"][t], p["u"][tc], b, t, k_past.shape[1], tc)

    merged = _merge(y_ssd.reshape(m, D_INNER), y_sb.reshape(m, D_ATTN), p["w_br_ssd"], p["w_br_attn"], gate,
                    min(m, 512), 512)
    x1 = _outproj_ln(merged, p["w_out"], x2, p["ln1_g"], p["ln1_b"], min(m, 512))
    x_out = _mlp_ln(x1, p["w_up"], p["b_up"], p["w_down"], p["b_down"], p["ln2_g"], p["ln2_b"], min(m, 512), 1024)
    kv_shape = (b, t, SB_HEADS, SB_HEADDIM)
    return x_out.reshape(b, t, D_MODEL), conv_new, h_new, k.reshape(kv_shape), v.reshape(kv_shape)


def _strict_lower(n):
    return jnp.tril(jnp.ones((n, n), F32), -1).astype(BF16)


def _layer_params(l, w_in, b_gate, conv_w, conv_b, dt_bias, a_log, d_skip, ssm_norm_w, w_br_ssd, w_br_attn, w_out,
                  ln1_g, ln1_b, w_up, b_up, w_down, b_down, ln2_g, ln2_b):
    w = w_in[l]
    o_dt = D_INNER + CONV_DIM
    o_q = o_dt + SSD_HEADS
    w_main = jnp.concatenate([w[:, :o_dt], w[:, o_q:]], axis=1).astype(BF16)
    w_dt = jnp.pad(w[:, o_dt:o_q], ((0, 0), (0, 128 - SSD_HEADS))).astype(BF16)
    row = lambda a: a[l].reshape(1, -1)
    head_of_lane = jnp.arange(D_INNER) // SSD_HEADDIM
    return {
        "w_main": w_main,
        "w_dt": w_dt,
        "b_gate": row(b_gate),
        "conv_w": conv_w[l],
        "conv_b": row(conv_b),
        "dt_bias": row(dt_bias),
        "a_exp": (-jnp.exp(a_log[l].astype(F32)))[head_of_lane].reshape(1, D_INNER),
        "dskip_exp": d_skip[l][head_of_lane].reshape(1, D_INNER),
        "norm_w": row(ssm_norm_w),
        "e_mat": (jnp.arange(SSD_HEADS)[:, None] == head_of_lane[None, :]).astype(BF16),
        "tri": jnp.tril(jnp.ones((CHUNK, CHUNK), F32)),
        "u": {n: _strict_lower(n) for n in (64, 256)},
        "w_br_ssd": w_br_ssd[l].astype(BF16),
        "w_br_attn": w_br_attn[l].astype(BF16),
        "w_out": w_out[l].astype(BF16),
        "ln1_g": row(ln1_g),
        "ln1_b": row(ln1_b),
        "w_up": w_up[l].astype(BF16),
        "b_up": row(b_up),
        "w_down": w_down[l].astype(BF16),
        "b_down": row(b_down),
        "ln2_g": row(ln2_g),
        "ln2_b": row(ln2_b),
    }


def kernel(x_prompt, x_sample, cache_conv, state_ssm, cache_k, cache_v, w_in, b_gate, conv_w, conv_b, dt_bias, a_log, d_skip, ssm_norm_w, w_br_ssd, w_br_attn, w_out, ln1_g, ln1_b, w_up, b_up, w_down, b_down, ln2_g, ln2_b):
    hp, hs = x_prompt, x_sample
    outs_p, outs_s = [], []
    for l in range(DEPTH):
        p = _layer_params(l, w_in, b_gate, conv_w, conv_b, dt_bias, a_log, d_skip, ssm_norm_w, w_br_ssd, w_br_attn,
                          w_out, ln1_g, ln1_b, w_up, b_up, w_down, b_down, ln2_g, ln2_b)
        hp, c1, s1, k1, v1 = _trunk(hp, None, None, None, None, p)
        outs_p.append((c1, s1, k1, v1))
        hs, c2, s2, k2, v2 = _trunk(hs, cache_conv[l], state_ssm[l], cache_k[l], cache_v[l], p)
        outs_s.append((c2, s2, k2, v2))
    stack = lambda outs, i: jnp.stack([o[i] for o in outs])
    return (hp, hs, stack(outs_p, 0), stack(outs_p, 1), stack(outs_p, 2), stack(outs_p, 3),
            stack(outs_s, 0), stack(outs_s, 1), stack(outs_s, 2), stack(outs_s, 3))
```

```python
import functools
import math

import jax
import jax.numpy as jnp
from jax import lax
from jax.experimental import pallas as pl
from jax.experimental.pallas import tpu as pltpu

D_MODEL = 2048
D_INNER = 2 * D_MODEL
SSD_HEADDIM = 64
SSD_HEADS = D_INNER // SSD_HEADDIM
SSD_GROUPS = 8
HEADS_PER_GROUP = SSD_HEADS // SSD_GROUPS
GROUP_W = D_INNER // SSD_GROUPS
D_STATE = 128
CONV_W = 4
CONV_DIM = D_INNER + 2 * SSD_GROUPS * D_STATE
SB_HEADDIM = 128
SB_HEADS = D_MODEL // SB_HEADDIM
D_ATTN = SB_HEADS * SB_HEADDIM
SB_SCALE = 1.0 / math.sqrt(SB_HEADDIM)
D_FF = 4 * D_MODEL
DEPTH = 1
ALPHA = (2.0 * DEPTH) ** 0.25
LOG2E = math.log2(math.e)
LN_EPS = 1e-5
RMS_EPS = 1e-5
CHUNK = 64
ATTN_HEADS_PER_STEP = 4
SSD_GROUP_UNROLL = 4

V7X_VMEM_BYTES = 64 * 1024 * 1024
VMEM_LIMIT = 56 * 1024 * 1024

F32 = jnp.float32
BF16 = jnp.bfloat16


def _cparams(sem):
    return pltpu.CompilerParams(dimension_semantics=sem, vmem_limit_bytes=VMEM_LIMIT)


def _sigmoid(x):
    return 1.0 / (1.0 + jnp.exp(-x))


def _softplus(x):
    return jnp.maximum(x, 0.0) + jnp.log1p(jnp.exp(-jnp.abs(x)))


def _split_bf16(x):
    hi = x.astype(BF16)
    lo = (x - hi.astype(F32)).astype(BF16)
    return hi, lo


def _split3_bf16(x):
    hi = x.astype(BF16)
    r1 = x - hi.astype(F32)
    mid = r1.astype(BF16)
    lo = (r1 - mid.astype(F32)).astype(BF16)
    return hi, mid, lo


def _layer_norm(x, g, b):
    mu = jnp.mean(x, axis=-1, keepdims=True)
    xc = x - mu
    var = jnp.mean(xc * xc, axis=-1, keepdims=True)
    return xc * lax.rsqrt(var + LN_EPS) * g + b


def _inproj_kernel(x_ref, wa_ref, wb_ref, wdt_ref, bg_ref, z_ref, xbc_ref, q_ref, k_ref, v_ref, g_ref, dt_ref,
                   kb_ref, vb_ref, *, ends):
    j = pl.program_id(1)
    ez, exbc, eq, ek, ev, _ = ends

    def proj(w_ref):
        return jnp.dot(x_ref[...], w_ref[...], preferred_element_type=F32)

    @pl.when(j == 0)
    def _():
        dt_ref[...] = jnp.dot(x_ref[...], wdt_ref[...], preferred_element_type=F32)

    @pl.when(j < ez)
    def _():
        z_ref[...] = proj(wa_ref)

    @pl.when((j >= ez) & (j < exbc))
    def _():
        xbc_ref[...] = proj(wa_ref)

    @pl.when((j >= exbc) & (j < eq))
    def _():
        q_ref[...] = (proj(wb_ref) * (SB_SCALE * LOG2E)).astype(BF16)

    @pl.when((j >= eq) & (j < ek))
    def _():
        acc = proj(wb_ref)
        k_ref[...] = acc
        kb_ref[...] = acc.astype(BF16)

    @pl.when((j >= ek) & (j < ev))
    def _():
        acc = proj(wb_ref)
        v_ref[...] = acc
        vb_ref[...] = acc.astype(BF16)

    @pl.when(j >= ev)
    def _():
        g_ref[...] = _sigmoid(proj(wb_ref) + bg_ref[...])


def _in_proj(xb, w_a, w_b, w_dt, b_gate, tm, tn):
    m = xb.shape[0]
    widths = (D_INNER, CONV_DIM, D_ATTN, D_ATTN, D_ATTN, 2 * D_MODEL)
    ends, e = [], 0
    for w in widths:
        assert w % tn == 0
        e += w // tn
        ends.append(e)
    starts = [0] + ends[:-1]
    nblk = [w // tn for w in widths]

    def seg_map(s, n):
        return lambda i, j: (i, jnp.clip(j - s, 0, n - 1))

    out_shapes = (
        jax.ShapeDtypeStruct((m, D_INNER), F32),
        jax.ShapeDtypeStruct((m, CONV_DIM), F32),
        jax.ShapeDtypeStruct((m, D_ATTN), BF16),
        jax.ShapeDtypeStruct((m, D_ATTN), F32),
        jax.ShapeDtypeStruct((m, D_ATTN), F32),
        jax.ShapeDtypeStruct((m, 2 * D_MODEL), F32),
        jax.ShapeDtypeStruct((m, 128), F32),
        jax.ShapeDtypeStruct((m, D_ATTN), BF16),
        jax.ShapeDtypeStruct((m, D_ATTN), BF16),
    )
    out_specs = [pl.BlockSpec((tm, tn), seg_map(starts[s], nblk[s])) for s in range(6)]
    out_specs.append(pl.BlockSpec((tm, 128), lambda i, j: (i, 0)))
    out_specs += [pl.BlockSpec((tm, tn), seg_map(starts[s], nblk[s])) for s in (3, 4)]
    return pl.pallas_call(
        functools.partial(_inproj_kernel, ends=tuple(ends)),
        grid=(m // tm, ends[-1]),
        in_specs=[
            pl.BlockSpec((tm, D_MODEL), lambda i, j: (i, 0)),
            pl.BlockSpec((D_MODEL, tn), lambda i, j: (0, jnp.minimum(j, ends[1] - 1))),
            pl.BlockSpec((D_MODEL, tn), lambda i, j: (0, jnp.maximum(j - ends[1], 0))),
            pl.BlockSpec((D_MODEL, 128), lambda i, j: (0, 0)),
            pl.BlockSpec((1, tn), lambda i, j: (0, jnp.clip(j - starts[5], 0, nblk[5] - 1))),
        ],
        out_specs=out_specs,
        out_shape=out_shapes,
        compiler_params=_cparams(("parallel", "arbitrary")),
        name="in_proj",
    )(xb, w_a, w_b, w_dt, b_gate)


def _ssd_kernel(*refs, has_history):
    if has_history:
        (xbc_ref, dtr_ref, z_ref, cprev_ref, h0_ref, convw_ref, convb_ref, dtb_ref, aexp_ref, dskip_ref,
         nw_ref, e_ref, tri_ref, y_ref, ht_ref, xpad, state) = refs
    else:
        (xbc_ref, dtr_ref, z_ref, convw_ref, convb_ref, dtb_ref, aexp_ref, dskip_ref,
         nw_ref, e_ref, tri_ref, y_ref, ht_ref, xpad, state) = refs
    c = pl.program_id(1)
    nc = pl.num_programs(1)
    L = CHUNK

    @pl.when(c == 0)
    def _():
        if has_history:
            xpad[0:8, :] = cprev_ref[0]
            state[...] = h0_ref[0]
        else:
            xpad[0:8, :] = jnp.zeros((8, CONV_DIM), F32)
            state[...] = jnp.zeros(state.shape, F32)

    xpad[8:8 + L, :] = xbc_ref[0]

    dt = _softplus(dtr_ref[0][:, :SSD_HEADS] + dtb_ref[...])
    dt_hilo = jnp.concatenate(_split_bf16(dt), axis=1)
    tri3 = tri_ref[...]
    row = lax.broadcasted_iota(jnp.int32, (L, GROUP_W), 0)
    lane = lax.broadcasted_iota(jnp.int32, (L, GROUP_W), 1)
    pos_in_head = lane & (SSD_HEADDIM - 1)
    diag_sel = row == pos_in_head
    causal = row >= pos_in_head
    lane128 = lax.broadcasted_iota(jnp.int32, (L, 128), 1)
    lo_half = lane128 < SSD_HEADDIM

    def conv_cols(start, width):
        sl = pl.ds(start, width)
        acc = convb_ref[:, sl] + convw_ref[CONV_W - 1:CONV_W, sl] * xpad[8:8 + L, sl]
        for t in range(CONV_W - 1):
            acc = acc + convw_ref[t:t + 1, sl] * xpad[5 + t:5 + t + L, sl]
        return acc * _sigmoid(acc)

    def groups_body(it, carry):
        gs = [it * SSD_GROUP_UNROLL + k for k in range(SSD_GROUP_UNROLL)]
        xoffs = [pl.multiple_of(g * GROUP_W, GROUP_W) for g in gs]
        xss, bgs, cgs = [], [], []
        for g, xoff in zip(gs, xoffs):
            boff = pl.multiple_of(D_INNER + g * D_STATE, D_STATE)
            coff = pl.multiple_of(D_INNER + SSD_GROUPS * D_STATE + g * D_STATE, D_STATE)
            xss.append(conv_cols(xoff, GROUP_W))
            bgs.append(conv_cols(boff, D_STATE).astype(BF16))
            cgs.append(conv_cols(coff, D_STATE).astype(BF16))

        dt_exps = [jnp.dot(dt_hilo, e_ref[:, pl.ds(xoff, GROUP_W)], preferred_element_type=F32) for xoff in xoffs]
        da_parts = [jnp.concatenate(_split3_bf16(dt_exp * aexp_ref[:, pl.ds(xoff, GROUP_W)]), axis=0)
                    for dt_exp, xoff in zip(dt_exps, xoffs)]
        acss = [jnp.dot(tri3, parts, preferred_element_type=F32) for parts in da_parts]
        cb2s = [lax.dot_general(cg, jnp.concatenate([bg, bg], axis=0), (((1,), (1,)), ((), ())),
                                preferred_element_type=F32) for cg, bg in zip(cgs, bgs)]

        zero = jnp.zeros((L, 128), BF16)
        eacss, a_lasts, xdds, mmats, bds = [], [], [], [], []
        for xs, dt_exp, acs, cb2 in zip(xss, dt_exps, acss, cb2s):
            a_last = acs[L - 1:L, :]
            a_lasts.append(a_last)
            eacss.append(jnp.exp(acs))
            xd = xs * dt_exp
            xdds.append((xd * jnp.exp(a_last - acs)).astype(BF16))
            xdb = xd.astype(BF16)
            rowform = jnp.sum(jnp.where(diag_sel, acs, 0.0), axis=0, keepdims=True)
            lmat = jnp.where(causal, jnp.exp(jnp.minimum(acs - rowform, 0.0)), 0.0)
            mmats.append((lmat * jnp.concatenate([cb2] * (GROUP_W // 128), axis=1)).astype(BF16))
            bds.append([jnp.concatenate([jnp.where(lo_half, xdb[:, 128 * p:128 * (p + 1)], zero),
                                         jnp.where(lo_half, zero, xdb[:, 128 * p:128 * (p + 1)])], axis=0)
                        for p in range(GROUP_W // 128)])

        sts = [state[g] for g in gs]
        ydiags = [jnp.concatenate([jnp.dot(mmat[:, 128 * p:128 * (p + 1)], bd[p], preferred_element_type=F32)
                                   for p in range(GROUP_W // 128)], axis=1) for mmat, bd in zip(mmats, bds)]
        yoffs = [jnp.dot(cg, st.astype(BF16), preferred_element_type=F32) for cg, st in zip(cgs, sts)]
        snews = [lax.dot_general(bg, xdd, (((0,), (0,)), ((), ())), preferred_element_type=F32)
                 for bg, xdd in zip(bgs, xdds)]

        for k, (g, xoff) in enumerate(zip(gs, xoffs)):
            cols = pl.ds(xoff, GROUP_W)
            y = ydiags[k] + yoffs[k] * eacss[k] + dskip_ref[:, cols] * xss[k]
            state[g] = sts[k] * jnp.exp(a_lasts[k]) + snews[k]
            zg = z_ref[0, :, cols]
            yg = y * (zg * _sigmoid(zg))
            ms = jnp.mean(yg * yg, axis=-1, keepdims=True)
            y_ref[0, :, cols] = (yg * lax.rsqrt(ms + RMS_EPS) * nw_ref[:, cols]).astype(y_ref.dtype)
        return carry

    lax.fori_loop(0, SSD_GROUPS // SSD_GROUP_UNROLL, groups_body, 0)
    xpad[0:8, :] = xpad[L:L + 8, :]

    @pl.when(c == nc - 1)
    def _():
        ht_ref[0] = state[...]


def _ssd(xbc, dtr, z, conv_prev8, h0t, conv_w, conv_b, dt_bias, a_exp, dskip_exp, norm_w, e_mat, tri, b, t):
    nc = t // CHUNK
    has_history = conv_prev8 is not None
    tok = lambda bi, c: (bi, c, 0)
    const2 = lambda bi, c: (0, 0)
    in_specs = [
        pl.BlockSpec((1, CHUNK, CONV_DIM), tok),
        pl.BlockSpec((1, CHUNK, 128), tok),
        pl.BlockSpec((1, CHUNK, D_INNER), tok),
    ]
    args = [xbc.reshape(b, t, CONV_DIM), dtr.reshape(b, t, 128), z.reshape(b, t, D_INNER)]
    if has_history:
        in_specs += [
            pl.BlockSpec((1, 8, CONV_DIM), lambda bi, c: (bi, 0, 0)),
            pl.BlockSpec((1, SSD_GROUPS, D_STATE, GROUP_W), lambda bi, c: (bi, 0, 0, 0)),
        ]
        args += [conv_prev8, h0t]
    in_specs += [
        pl.BlockSpec((CONV_W, CONV_DIM), const2),
        pl.BlockSpec((1, CONV_DIM), const2),
        pl.BlockSpec((1, SSD_HEADS), const2),
        pl.BlockSpec((1, D_INNER), const2),
        pl.BlockSpec((1, D_INNER), const2),
        pl.BlockSpec((1, D_INNER), const2),
        pl.BlockSpec((2 * SSD_HEADS, D_INNER), const2),
        pl.BlockSpec((CHUNK, 3 * CHUNK), const2),
    ]
    args += [conv_w, conv_b, dt_bias, a_exp, dskip_exp, norm_w, e_mat, tri]
    return pl.pallas_call(
        functools.partial(_ssd_kernel, has_history=has_history),
        grid=(b, nc),
        in_specs=in_specs,
        out_specs=[
            pl.BlockSpec((1, CHUNK, D_INNER), tok),
            pl.BlockSpec((1, SSD_GROUPS, D_STATE, GROUP_W), lambda bi, c: (bi, 0, 0, 0)),
        ],
        out_shape=(
            jax.ShapeDtypeStruct((b, t, D_INNER), BF16),
            jax.ShapeDtypeStruct((b, SSD_GROUPS, D_STATE, GROUP_W), F32),
        ),
        scratch_shapes=[
            pltpu.VMEM((8 + CHUNK, CONV_DIM), F32),
            pltpu.VMEM((SSD_GROUPS, D_STATE, GROUP_W), F32),
        ],
        compiler_params=_cparams(("parallel", "arbitrary")),
        name="conv_ssd",
    )(*args)


def _sb_tiles(qs, kbs, vbs, u, carry, masked):
    n = len(qs)
    ts = [lax.dot_general(qs[h], kbs[h], (((1,), (1,)), ((), ())), preferred_element_type=F32) for h in range(n)]
    if masked:
        tq, tk = ts[0].shape
        visible = lax.broadcasted_iota(jnp.int32, (tq, tk), 1) < lax.broadcasted_iota(jnp.int32, (tq, tk), 0)
    sp2s, hilos = [], []
    for t in ts:
        neg_abs = lax.bitcast_convert_type(lax.bitcast_convert_type(t, jnp.uint32) | jnp.uint32(0x80000000), F32)
        sp2 = jnp.maximum(t, 0.0) + jnp.log(1.0 + jnp.exp2(neg_abs)) * LOG2E
        if masked:
            sp2 = jnp.where(visible, sp2, 0.0)
        sp2s.append(sp2)
        hilos.append(jnp.concatenate(_split_bf16(sp2), axis=1))
    sums = [jnp.dot(hl, u, preferred_element_type=F32) for hl in hilos]
    ws = []
    for t, s in zip(ts, sums):
        w = jnp.exp2(t - s)
        if masked:
            w = jnp.where(visible, w, 0.0)
        ws.append(w.astype(BF16))
    pvs = [jnp.dot(ws[h], vbs[h], preferred_element_type=F32) for h in range(n)]
    out = []
    for h in range(n):
        r, acc = carry[h]
        out.append((r + jnp.sum(sp2s[h], axis=-1, keepdims=True), acc + jnp.exp2(-r) * pvs[h]))
    return tuple(out)


def _head_cols(hh):
    return slice(hh * SB_HEADDIM, (hh + 1) * SB_HEADDIM)


def _attn_prompt_kernel(q_ref, k_ref, v_ref, u_ref, o_ref, *, tq, hp):
    qi = pl.program_id(2)
    u = u_ref[...]
    qs = [q_ref[0, :, _head_cols(hh)] for hh in range(hp)]

    def tiles(jb, carry, masked):
        off = pl.multiple_of(jb * tq, tq)
        kbs = [k_ref[0, pl.ds(off, tq), _head_cols(hh)] for hh in range(hp)]
        vbs = [v_ref[0, pl.ds(off, tq), _head_cols(hh)] for hh in range(hp)]
        return _sb_tiles(qs, kbs, vbs, u, carry, masked)

    init = tuple((jnp.zeros((tq, 1), F32), jnp.zeros((tq, SB_HEADDIM), F32)) for _ in range(hp))
    carry = tiles(qi, init, True)
    carry = lax.fori_loop(0, qi, lambda i, c: tiles(qi - 1 - i, c, False), carry)
    for hh in range(hp):
        o_ref[0, :, _head_cols(hh)] = carry[hh][1].astype(o_ref.dtype)


def _attn_prompt(qs, kb, vb, u, b, t, tq, hp):
    q3, k3, v3 = qs.reshape(b, t, D_ATTN), kb.reshape(b, t, D_ATTN), vb.reshape(b, t, D_ATTN)
    w = hp * SB_HEADDIM
    return pl.pallas_call(
        functools.partial(_attn_prompt_kernel, tq=tq, hp=hp),
        grid=(b, SB_HEADS // hp, t // tq),
        in_specs=[
            pl.BlockSpec((1, tq, w), lambda bi, h, qi: (bi, qi, h)),
            pl.BlockSpec((1, t, w), lambda bi, h, qi: (bi, 0, h)),
            pl.BlockSpec((1, t, w), lambda bi, h, qi: (bi, 0, h)),
            pl.BlockSpec((2 * tq, tq), lambda bi, h, qi: (0, 0)),
        ],
        out_specs=pl.BlockSpec((1, tq, w), lambda bi, h, qi: (bi, qi, h)),
        out_shape=jax.ShapeDtypeStruct((b, t, D_ATTN), BF16),
        compiler_params=_cparams(("parallel", "parallel", "arbitrary")),
        name="sb_attn_prompt",
    )(q3, k3, v3, u)


def _attn_sample_kernel(q_ref, kn_ref, vn_ref, kc_ref, vc_ref, un_ref, uc_ref, o_ref, *, tn, tc, n_cache, hp, hb):
    un, uc = un_ref[...], uc_ref[...]
    for h0 in range(0, hb, hp):
        heads = range(h0, h0 + hp)
        qs = [q_ref[0, :, _head_cols(hh)] for hh in heads]
        init = tuple((jnp.zeros((tn, 1), F32), jnp.zeros((tn, SB_HEADDIM), F32)) for _ in heads)
        carry = _sb_tiles(qs, [kn_ref[0, :, _head_cols(hh)] for hh in heads],
                          [vn_ref[0, :, _head_cols(hh)] for hh in heads], un, init, True)

        def body(i, carry, heads=heads, qs=qs):
            off = pl.multiple_of((n_cache - 1 - i) * tc, tc)
            kbs = [kc_ref[0, pl.ds(off, tc), hh, :].astype(BF16) for hh in heads]
            vbs = [vc_ref[0, pl.ds(off, tc), hh, :].astype(BF16) for hh in heads]
            return _sb_tiles(qs, kbs, vbs, uc, carry, False)

        carry = lax.fori_loop(0, n_cache, body, carry)
        for hh, (_, acc) in zip(heads, carry):
            o_ref[0, :, _head_cols(hh)] = acc.astype(o_ref.dtype)


def _attn_sample(qs, kb, vb, cache_k, cache_v, un, uc, b, t, past, tc, hp):
    q3, k3, v3 = qs.reshape(b, t, D_ATTN), kb.reshape(b, t, D_ATTN), vb.reshape(b, t, D_ATTN)
    hb = 8
    w = hb * SB_HEADDIM
    new = lambda bi, h: (bi, 0, h)
    cache = lambda bi, h: (bi, 0, h, 0)
    return pl.pallas_call(
        functools.partial(_attn_sample_kernel, tn=t, tc=tc, n_cache=past // tc, hp=hp, hb=hb),
        grid=(b, SB_HEADS // hb),
        in_specs=[
            pl.BlockSpec((1, t, w), new),
            pl.BlockSpec((1, t, w), new),
            pl.BlockSpec((1, t, w), new),
            pl.BlockSpec((1, past, hb, SB_HEADDIM), cache),
            pl.BlockSpec((1, past, hb, SB_HEADDIM), cache),
            pl.BlockSpec((2 * t, t), lambda bi, h: (0, 0)),
            pl.BlockSpec((2 * tc, tc), lambda bi, h: (0, 0)),
        ],
        out_specs=pl.BlockSpec((1, t, w), new),
        out_shape=jax.ShapeDtypeStruct((b, t, D_ATTN), BF16),
        compiler_params=_cparams(("parallel", "parallel")),
        name="sb_attn_sample",
    )(q3, k3, v3, cache_k, cache_v, un, uc)


def _merge_kernel(ya_ref, yb_ref, wa_ref, wb_ref, ga_ref, gb_ref, o_ref):
    a = jnp.dot(ya_ref[...], wa_ref[...], preferred_element_type=F32)
    bb = jnp.dot(yb_ref[...], wb_ref[...], preferred_element_type=F32)
    o_ref[...] = (ga_ref[...] * a + gb_ref[...] * bb).astype(o_ref.dtype)


def _merge(ya, yb, wa, wb, gate, tm, tn):
    m = ya.shape[0]
    nb = D_MODEL // tn
    return pl.pallas_call(
        _merge_kernel,
        grid=(m // tm, nb),
        in_specs=[
            pl.BlockSpec((tm, D_INNER), lambda i, j: (i, 0)),
            pl.BlockSpec((tm, D_ATTN), lambda i, j: (i, 0)),
            pl.BlockSpec((D_INNER, tn), lambda i, j: (0, j)),
            pl.BlockSpec((D_ATTN, tn), lambda i, j: (0, j)),
            pl.BlockSpec((tm, tn), lambda i, j: (i, j)),
            pl.BlockSpec((tm, tn), lambda i, j: (i, j + nb)),
        ],
        out_specs=pl.BlockSpec((tm, tn), lambda i, j: (i, j)),
        out_shape=jax.ShapeDtypeStruct((m, D_MODEL), BF16),
        compiler_params=_cparams(("parallel", "arbitrary")),
        name="branch_merge",
    )(ya, yb, wa, wb, gate, gate)


def _outproj_kernel(m_ref, w_ref, x_ref, g_ref, b_ref, o_ref):
    mix = jnp.dot(m_ref[...], w_ref[...], preferred_element_type=F32)
    o_ref[...] = _layer_norm(ALPHA * x_ref[...] + mix, g_ref[...], b_ref[...])


def _outproj_ln(merged, w_out, x, g, bb, tm):
    m = x.shape[0]
    row = lambda i: (i, 0)
    const = lambda i: (0, 0)
    return pl.pallas_call(
        _outproj_kernel,
        grid=(m // tm,),
        in_specs=[
            pl.BlockSpec((tm, D_MODEL), row),
            pl.BlockSpec((D_MODEL, D_MODEL), const),
            pl.BlockSpec((tm, D_MODEL), row),
            pl.BlockSpec((1, D_MODEL), const),
            pl.BlockSpec((1, D_MODEL), const),
        ],
        out_specs=pl.BlockSpec((tm, D_MODEL), row),
        out_shape=jax.ShapeDtypeStruct((m, D_MODEL), F32),
        compiler_params=_cparams(("parallel",)),
        name="outproj_ln",
    )(merged, w_out, x, g, bb)


def _mlp_kernel(x_ref, wu_ref, bu_ref, wd_ref, bd_ref, g_ref, b_ref, o_ref, xb, acc):
    f = pl.program_id(1)

    @pl.when(f == 0)
    def _():
        xb[...] = x_ref[...].astype(BF16)
        acc[...] = jnp.zeros(acc.shape, F32)

    h = jnp.dot(xb[...], wu_ref[...], preferred_element_type=F32) + bu_ref[...]
    h = jnp.square(jnp.maximum(h, 0.0)).astype(BF16)
    acc[...] += jnp.dot(h, wd_ref[...], preferred_element_type=F32)

    @pl.when(f == pl.num_programs(1) - 1)
    def _():
        o_ref[...] = _layer_norm(ALPHA * x_ref[...] + acc[...] + bd_ref[...], g_ref[...], b_ref[...])


def _mlp_ln(x1, w_up, b_up, w_down, b_down, g, bb, tm, tf):
    m = x1.shape[0]
    row = lambda i, f: (i, 0)
    const = lambda i, f: (0, 0)
    return pl.pallas_call(
        _mlp_kernel,
        grid=(m // tm, D_FF // tf),
        in_specs=[
            pl.BlockSpec((tm, D_MODEL), row),
            pl.BlockSpec((D_MODEL, tf), lambda i, f: (0, f)),
            pl.BlockSpec((1, tf), lambda i, f: (0, f)),
            pl.BlockSpec((tf, D_MODEL), lambda i, f: (f, 0)),
            pl.BlockSpec((1, D_MODEL), const),
            pl.BlockSpec((1, D_MODEL), const),
            pl.BlockSpec((1, D_MODEL), const),
        ],
        out_specs=pl.BlockSpec((tm, D_MODEL), row),
        out_shape=jax.ShapeDtypeStruct((m, D_MODEL), F32),
        scratch_shapes=[pltpu.VMEM((tm, D_MODEL), BF16), pltpu.VMEM((tm, D_MODEL), F32)],
        compiler_params=_cparams(("parallel", "arbitrary")),
        name="mlp_ln",
    )(x1, w_up, b_up, w_down, b_down, g, bb)


def _trunk(x, conv_prev, h0, k_past, v_past, p):
    b, t, _ = x.shape
    m = b * t
    x2 = x.reshape(m, D_MODEL)
    tm = min(m, 1024)
    z, xbc, q, k, v, gate, dtr, kb, vb = _in_proj(x2.astype(BF16), p["w_a"], p["w_b"], p["w_dt"], p["b_gate"], tm, 512)

    if conv_prev is None:
        conv_prev8, h0t = None, None
    else:
        conv_prev8 = jnp.pad(conv_prev, ((0, 0), (8 - (CONV_W - 1), 0), (0, 0)))
        h0t = h0.reshape(b, SSD_GROUPS, GROUP_W, D_STATE).transpose(0, 1, 3, 2)
    y_ssd, ht = _ssd(xbc, dtr, z, conv_prev8, h0t, p["conv_w"], p["conv_b"], p["dt_bias"], p["a_exp"],
                     p["dskip_exp"], p["norm_w"], p["e_mat"], p["tri"], b, t)
    h_new = ht.transpose(0, 1, 3, 2).reshape(b, SSD_HEADS, SSD_HEADDIM, D_STATE)
    conv_new = xbc.reshape(b, t, CONV_DIM)[:, t - (CONV_W - 1):, :]

    if k_past is None:
        tq = 256
        y_sb = _attn_prompt(q, kb, vb, p["u"][tq], b, t, tq, ATTN_HEADS_PER_STEP)
    else:
        tc = 256
        y_sb = _attn_sample(q, kb, vb, k_past, v_past, p["u"][t], p["u"][tc], b, t, k_past.shape[1], tc,
                            ATTN_HEADS_PER_STEP)

    merged = _merge(y_ssd.reshape(m, D_INNER), y_sb.reshape(m, D_ATTN), p["w_br_ssd"], p["w_br_attn"], gate,
                    min(m, 512), 512)
    x1 = _outproj_ln(merged, p["w_out"], x2, p["ln1_g"], p["ln1_b"], min(m, 512))
    x_out = _mlp_ln(x1, p["w_up"], p["b_up"], p["w_down"], p["b_down"], p["ln2_g"], p["ln2_b"], min(m, 512), 1024)
    kv_shape = (b, t, SB_HEADS, SB_HEADDIM)
    return x_out.reshape(b, t, D_MODEL), conv_new, h_new, k.reshape(kv_shape), v.reshape(kv_shape)


def _suffix_sum_matrix(n):
    u = jnp.tril(jnp.ones((n, n), F32)).astype(BF16)
    return jnp.concatenate([u, u], axis=0)


def _layer_params(l, w_in, b_gate, conv_w, conv_b, dt_bias, a_log, d_skip, ssm_norm_w, w_br_ssd, w_br_attn, w_out,
                  ln1_g, ln1_b, w_up, b_up, w_down, b_down, ln2_g, ln2_b):
    w = w_in[l]
    o_dt = D_INNER + CONV_DIM
    o_q = o_dt + SSD_HEADS
    w_dt = jnp.pad(w[:, o_dt:o_q], ((0, 0), (0, 128 - SSD_HEADS))).astype(BF16)
    row = lambda a: a[l].reshape(1, -1)
    head_of_lane = jnp.arange(D_INNER) // SSD_HEADDIM
    return {
        "w_a": w[:, :o_dt].astype(BF16),
        "w_b": w[:, o_q:].astype(BF16),
        "w_dt": w_dt,
        "b_gate": row(b_gate),
        "conv_w": conv_w[l],
        "conv_b": row(conv_b),
        "dt_bias": row(dt_bias),
        "a_exp": jnp.repeat(-jnp.exp(a_log[l].astype(F32)), SSD_HEADDIM).reshape(1, D_INNER),
        "dskip_exp": jnp.repeat(d_skip[l], SSD_HEADDIM).reshape(1, D_INNER),
        "norm_w": row(ssm_norm_w),
        "e_mat": jnp.tile(jnp.arange(SSD_HEADS)[:, None] == head_of_lane[None, :], (2, 1)).astype(BF16),
        "tri": jnp.tile(jnp.tril(jnp.ones((CHUNK, CHUNK), F32)), (1, 3)).astype(BF16),
        "u": {n: _suffix_sum_matrix(n) for n in (64, 256)},
        "w_br_ssd": w_br_ssd[l].astype(BF16),
        "w_br_attn": w_br_attn[l].astype(BF16),
        "w_out": w_out[l].astype(BF16),
        "ln1_g": row(ln1_g),
        "ln1_b": row(ln1_b),
        "w_up": w_up[l].astype(BF16),
        "b_up": row(b_up),
        "w_down": w_down[l].astype(BF16),
        "b_down": row(b_down),
        "ln2_g": row(ln2_g),
        "ln2_b": row(ln2_b),
    }


def kernel(x_prompt, x_sample, cache_conv, state_ssm, cache_k, cache_v, w_in, b_gate, conv_w, conv_b, dt_bias, a_log, d_skip, ssm_norm_w, w_br_ssd, w_br_attn, w_out, ln1_g, ln1_b, w_up, b_up, w_down, b_down, ln2_g, ln2_b):
    hp, hs = x_prompt, x_sample
    outs_p, outs_s = [], []
    for l in range(DEPTH):
        p = _layer_params(l, w_in, b_gate, conv_w, conv_b, dt_bias, a_log, d_skip, ssm_norm_w, w_br_ssd, w_br_attn,
                          w_out, ln1_g, ln1_b, w_up, b_up, w_down, b_down, ln2_g, ln2_b)
        hp, c1, s1, k1, v1 = _trunk(hp, None, None, None, None, p)
        outs_p.append((c1, s1, k1, v1))
        hs, c2, s2, k2, v2 = _trunk(hs, cache_conv[l], state_ssm[l], cache_k[l], cache_v[l], p)
        outs_s.append((c2, s2, k2, v2))
    stack = lambda outs, i: jnp.stack([o[i] for o in outs])
    return (hp, hs, stack(outs_p, 0), stack(outs_p, 1), stack(outs_p, 2), stack(outs_p, 3),
            stack(outs_s, 0), stack(outs_s, 1), stack(outs_s, 2), stack(outs_s, 3))
```

```python
import functools
import math

import jax
import jax.numpy as jnp
from jax import lax
from jax.experimental import pallas as pl
from jax.experimental.pallas import tpu as pltpu

D_MODEL = 2048
D_INNER = 2 * D_MODEL
SSD_HEADDIM = 64
SSD_HEADS = D_INNER // SSD_HEADDIM
SSD_GROUPS = 8
HEADS_PER_GROUP = SSD_HEADS // SSD_GROUPS
GROUP_W = D_INNER // SSD_GROUPS
D_STATE = 128
CONV_W = 4
CONV_DIM = D_INNER + 2 * SSD_GROUPS * D_STATE
SB_HEADDIM = 128
SB_HEADS = D_MODEL // SB_HEADDIM
D_ATTN = SB_HEADS * SB_HEADDIM
SB_SCALE = 1.0 / math.sqrt(SB_HEADDIM)
D_FF = 4 * D_MODEL
DEPTH = 1
ALPHA = (2.0 * DEPTH) ** 0.25
LOG2E = math.log2(math.e)
SP2_CLAMP = 126.0
R_STOP = 160.0
LN_EPS = 1e-5
RMS_EPS = 1e-5
CHUNK = 64
ATTN_HEADS_PER_STEP = 4
SSD_GROUP_UNROLL = 4

V7X_VMEM_BYTES = 64 * 1024 * 1024
VMEM_LIMIT = 56 * 1024 * 1024

F32 = jnp.float32
BF16 = jnp.bfloat16


def _cparams(sem):
    return pltpu.CompilerParams(dimension_semantics=sem, vmem_limit_bytes=VMEM_LIMIT)


def _sigmoid(x):
    return 1.0 / (1.0 + jnp.exp(-x))


def _softplus(x):
    return jnp.maximum(x, 0.0) + jnp.log1p(jnp.exp(-jnp.abs(x)))


def _split_bf16(x):
    hi = x.astype(BF16)
    lo = (x - hi.astype(F32)).astype(BF16)
    return hi, lo


def _split3_bf16(x):
    hi = x.astype(BF16)
    r1 = x - hi.astype(F32)
    mid = r1.astype(BF16)
    lo = (r1 - mid.astype(F32)).astype(BF16)
    return hi, mid, lo


def _layer_norm(x, g, b):
    mu = jnp.mean(x, axis=-1, keepdims=True)
    xc = x - mu
    var = jnp.mean(xc * xc, axis=-1, keepdims=True)
    return xc * lax.rsqrt(var + LN_EPS) * g + b


def _inproj_kernel(x_ref, wa_ref, wb_ref, wdt_ref, bg_ref, z_ref, xbc_ref, q_ref, k_ref, v_ref, g_ref, dt_ref,
                   kb_ref, vb_ref, *, ends):
    j = pl.program_id(1)
    ez, exbc, eq, ek, ev, _ = ends

    def proj(w_ref):
        return jnp.dot(x_ref[...], w_ref[...], preferred_element_type=F32)

    @pl.when(j == 0)
    def _():
        dt_ref[...] = jnp.dot(x_ref[...], wdt_ref[...], preferred_element_type=F32)

    @pl.when(j < ez)
    def _():
        acc = proj(wa_ref)
        z_ref[...] = acc * _sigmoid(acc)

    @pl.when((j >= ez) & (j < exbc))
    def _():
        xbc_ref[...] = proj(wa_ref)

    @pl.when((j >= exbc) & (j < eq))
    def _():
        q_ref[...] = (proj(wb_ref) * (SB_SCALE * LOG2E)).astype(BF16)

    @pl.when((j >= eq) & (j < ek))
    def _():
        acc = proj(wb_ref)
        k_ref[...] = acc
        kb_ref[...] = acc.astype(BF16)

    @pl.when((j >= ek) & (j < ev))
    def _():
        acc = proj(wb_ref)
        v_ref[...] = acc
        vb_ref[...] = acc.astype(BF16)

    @pl.when(j >= ev)
    def _():
        g_ref[...] = _sigmoid(proj(wb_ref) + bg_ref[...])


def _in_proj(xb, w_a, w_b, w_dt, b_gate, tm, tn):
    m = xb.shape[0]
    widths = (D_INNER, CONV_DIM, D_ATTN, D_ATTN, D_ATTN, 2 * D_MODEL)
    ends, e = [], 0
    for w in widths:
        assert w % tn == 0
        e += w // tn
        ends.append(e)
    starts = [0] + ends[:-1]
    nblk = [w // tn for w in widths]

    def seg_map(s, n):
        return lambda i, j: (i, jnp.clip(j - s, 0, n - 1))

    out_shapes = (
        jax.ShapeDtypeStruct((m, D_INNER), F32),
        jax.ShapeDtypeStruct((m, CONV_DIM), F32),
        jax.ShapeDtypeStruct((m, D_ATTN), BF16),
        jax.ShapeDtypeStruct((m, D_ATTN), F32),
        jax.ShapeDtypeStruct((m, D_ATTN), F32),
        jax.ShapeDtypeStruct((m, 2 * D_MODEL), F32),
        jax.ShapeDtypeStruct((m, 128), F32),
        jax.ShapeDtypeStruct((m, D_ATTN), BF16),
        jax.ShapeDtypeStruct((m, D_ATTN), BF16),
    )
    out_specs = [pl.BlockSpec((tm, tn), seg_map(starts[s], nblk[s])) for s in range(6)]
    out_specs.append(pl.BlockSpec((tm, 128), lambda i, j: (i, 0)))
    out_specs += [pl.BlockSpec((tm, tn), seg_map(starts[s], nblk[s])) for s in (3, 4)]
    return pl.pallas_call(
        functools.partial(_inproj_kernel, ends=tuple(ends)),
        grid=(m // tm, ends[-1]),
        in_specs=[
            pl.BlockSpec((tm, D_MODEL), lambda i, j: (i, 0)),
            pl.BlockSpec((D_MODEL, tn), lambda i, j: (0, jnp.minimum(j, ends[1] - 1))),
            pl.BlockSpec((D_MODEL, tn), lambda i, j: (0, jnp.maximum(j - ends[1], 0))),
            pl.BlockSpec((D_MODEL, 128), lambda i, j: (0, 0)),
            pl.BlockSpec((1, tn), lambda i, j: (0, jnp.clip(j - starts[5], 0, nblk[5] - 1))),
        ],
        out_specs=out_specs,
        out_shape=out_shapes,
        compiler_params=_cparams(("parallel", "arbitrary")),
        name="in_proj",
    )(xb, w_a, w_b, w_dt, b_gate)


def _ssd_kernel(*refs, has_history):
    if has_history:
        (xbc_ref, dtr_ref, z_ref, cprev_ref, h0_ref, convw_ref, convb_ref, dtb_ref, aexp_ref, dskip_ref,
         nw_ref, e_ref, tri_ref, y_ref, ht_ref, xpad, state) = refs
    else:
        (xbc_ref, dtr_ref, z_ref, convw_ref, convb_ref, dtb_ref, aexp_ref, dskip_ref,
         nw_ref, e_ref, tri_ref, y_ref, ht_ref, xpad, state) = refs
    c = pl.program_id(1)
    nc = pl.num_programs(1)
    L = CHUNK

    @pl.when(c == 0)
    def _():
        if has_history:
            xpad[0:8, :] = cprev_ref[0]
            state[...] = h0_ref[0]
        else:
            xpad[0:8, :] = jnp.zeros((8, CONV_DIM), F32)
            state[...] = jnp.zeros(state.shape, F32)

    xpad[8:8 + L, :] = xbc_ref[0]

    dt = _softplus(dtr_ref[0][:, :SSD_HEADS] + dtb_ref[...])
    dt_hilo = jnp.concatenate(_split_bf16(dt), axis=1)
    tri3 = tri_ref[...]
    row = lax.broadcasted_iota(jnp.int32, (L, GROUP_W), 0)
    lane = lax.broadcasted_iota(jnp.int32, (L, GROUP_W), 1)
    pos_in_head = lane & (SSD_HEADDIM - 1)
    diag_sel = row == pos_in_head
    causal = row >= pos_in_head
    lane128 = lax.broadcasted_iota(jnp.int32, (L, 128), 1)
    lo_half = lane128 < SSD_HEADDIM

    def conv_cols(start, width):
        sl = pl.ds(start, width)
        acc = convb_ref[:, sl] + convw_ref[CONV_W - 1:CONV_W, sl] * xpad[8:8 + L, sl]
        for t in range(CONV_W - 1):
            acc = acc + convw_ref[t:t + 1, sl] * xpad[5 + t:5 + t + L, sl]
        return acc * _sigmoid(acc)

    def groups_body(it, carry):
        gs = [it * SSD_GROUP_UNROLL + k for k in range(SSD_GROUP_UNROLL)]
        xoffs = [pl.multiple_of(g * GROUP_W, GROUP_W) for g in gs]
        xss, bgs, cgs = [], [], []
        for g, xoff in zip(gs, xoffs):
            boff = pl.multiple_of(D_INNER + g * D_STATE, D_STATE)
            coff = pl.multiple_of(D_INNER + SSD_GROUPS * D_STATE + g * D_STATE, D_STATE)
            xss.append(conv_cols(xoff, GROUP_W))
            bgs.append(conv_cols(boff, D_STATE).astype(BF16))
            cgs.append(conv_cols(coff, D_STATE).astype(BF16))

        dt_exps = [jnp.dot(dt_hilo, e_ref[:, pl.ds(xoff, GROUP_W)], preferred_element_type=F32) for xoff in xoffs]
        da_parts = [jnp.concatenate(_split3_bf16(dt_exp * aexp_ref[:, pl.ds(xoff, GROUP_W)]), axis=0)
                    for dt_exp, xoff in zip(dt_exps, xoffs)]
        acss = [jnp.dot(tri3, parts, preferred_element_type=F32) for parts in da_parts]
        cb2s = [lax.dot_general(cg, jnp.concatenate([bg, bg], axis=0), (((1,), (1,)), ((), ())),
                                preferred_element_type=F32) for cg, bg in zip(cgs, bgs)]

        zero = jnp.zeros((L, 128), BF16)
        eacss, a_lasts, xdds, mmats, bds = [], [], [], [], []
        for xs, dt_exp, acs, cb2 in zip(xss, dt_exps, acss, cb2s):
            a_last = acs[L - 1:L, :]
            a_lasts.append(a_last)
            eacss.append(jnp.exp(acs))
            xd = xs * dt_exp
            xdds.append((xd * jnp.exp(a_last - acs)).astype(BF16))
            xdb = xd.astype(BF16)
            rowform = jnp.sum(jnp.where(diag_sel, acs, 0.0), axis=0, keepdims=True)
            lmat = jnp.where(causal, jnp.exp(jnp.minimum(acs - rowform, 0.0)), 0.0)
            mmats.append((lmat * jnp.concatenate([cb2] * (GROUP_W // 128), axis=1)).astype(BF16))
            bds.append([jnp.concatenate([jnp.where(lo_half, xdb[:, 128 * p:128 * (p + 1)], zero),
                                         jnp.where(lo_half, zero, xdb[:, 128 * p:128 * (p + 1)])], axis=0)
                        for p in range(GROUP_W // 128)])

        sts = [state[g] for g in gs]
        ydiags = [jnp.concatenate([jnp.dot(mmat[:, 128 * p:128 * (p + 1)], bd[p], preferred_element_type=F32)
                                   for p in range(GROUP_W // 128)], axis=1) for mmat, bd in zip(mmats, bds)]
        yoffs = [jnp.dot(cg, st.astype(BF16), preferred_element_type=F32) for cg, st in zip(cgs, sts)]
        snews = [lax.dot_general(bg, xdd, (((0,), (0,)), ((), ())), preferred_element_type=F32)
                 for bg, xdd in zip(bgs, xdds)]

        for k, (g, xoff) in enumerate(zip(gs, xoffs)):
            cols = pl.ds(xoff, GROUP_W)
            y = ydiags[k] + yoffs[k] * eacss[k] + dskip_ref[:, cols] * xss[k]
            state[g] = sts[k] * jnp.exp(a_lasts[k]) + snews[k]
            yg = y * z_ref[0, :, cols]
            ms = jnp.mean(yg * yg, axis=-1, keepdims=True)
            y_ref[0, :, cols] = (yg * lax.rsqrt(ms + RMS_EPS) * nw_ref[:, cols]).astype(y_ref.dtype)
        return carry

    lax.fori_loop(0, SSD_GROUPS // SSD_GROUP_UNROLL, groups_body, 0)
    xpad[0:8, :] = xpad[L:L + 8, :]

    @pl.when(c == nc - 1)
    def _():
        ht_ref[0] = state[...]


def _ssd(xbc, dtr, z, conv_prev8, h0t, conv_w, conv_b, dt_bias, a_exp, dskip_exp, norm_w, e_mat, tri, b, t):
    nc = t // CHUNK
    has_history = conv_prev8 is not None
    tok = lambda bi, c: (bi, c, 0)
    const2 = lambda bi, c: (0, 0)
    in_specs = [
        pl.BlockSpec((1, CHUNK, CONV_DIM), tok),
        pl.BlockSpec((1, CHUNK, 128), tok),
        pl.BlockSpec((1, CHUNK, D_INNER), tok),
    ]
    args = [xbc.reshape(b, t, CONV_DIM), dtr.reshape(b, t, 128), z.reshape(b, t, D_INNER)]
    if has_history:
        in_specs += [
            pl.BlockSpec((1, 8, CONV_DIM), lambda bi, c: (bi, 0, 0)),
            pl.BlockSpec((1, SSD_GROUPS, D_STATE, GROUP_W), lambda bi, c: (bi, 0, 0, 0)),
        ]
        args += [conv_prev8, h0t]
    in_specs += [
        pl.BlockSpec((CONV_W, CONV_DIM), const2),
        pl.BlockSpec((1, CONV_DIM), const2),
        pl.BlockSpec((1, SSD_HEADS), const2),
        pl.BlockSpec((1, D_INNER), const2),
        pl.BlockSpec((1, D_INNER), const2),
        pl.BlockSpec((1, D_INNER), const2),
        pl.BlockSpec((2 * SSD_HEADS, D_INNER), const2),
        pl.BlockSpec((CHUNK, 3 * CHUNK), const2),
    ]
    args += [conv_w, conv_b, dt_bias, a_exp, dskip_exp, norm_w, e_mat, tri]
    return pl.pallas_call(
        functools.partial(_ssd_kernel, has_history=has_history),
        grid=(b, nc),
        in_specs=in_specs,
        out_specs=[
            pl.BlockSpec((1, CHUNK, D_INNER), tok),
            pl.BlockSpec((1, SSD_GROUPS, D_STATE, GROUP_W), lambda bi, c: (bi, 0, 0, 0)),
        ],
        out_shape=(
            jax.ShapeDtypeStruct((b, t, D_INNER), BF16),
            jax.ShapeDtypeStruct((b, SSD_GROUPS, D_STATE, GROUP_W), F32),
        ),
        scratch_shapes=[
            pltpu.VMEM((8 + CHUNK, CONV_DIM), F32),
            pltpu.VMEM((SSD_GROUPS, D_STATE, GROUP_W), F32),
        ],
        compiler_params=_cparams(("parallel", "arbitrary")),
        name="conv_ssd",
    )(*args)


def _sb_tiles(qs, kbs, vbs, u, carry, masked):
    n = len(qs)
    ts = [lax.dot_general(qs[h], kbs[h], (((1,), (1,)), ((), ())), preferred_element_type=F32) for h in range(n)]
    if masked:
        tq, tk = ts[0].shape
        visible = lax.broadcasted_iota(jnp.int32, (tq, tk), 1) < lax.broadcasted_iota(jnp.int32, (tq, tk), 0)
    sp2s, hilos = [], []
    for t in ts:
        sp2 = jnp.maximum(t, jnp.log(1.0 + jnp.exp2(jnp.minimum(t, SP2_CLAMP))) * LOG2E)
        if masked:
            sp2 = jnp.where(visible, sp2, 0.0)
        sp2s.append(sp2)
        hilos.append(jnp.concatenate(_split_bf16(sp2), axis=1))
    sums = [jnp.dot(hl, u, preferred_element_type=F32) for hl in hilos]
    ws = []
    for t, s in zip(ts, sums):
        w = jnp.exp2(t - s)
        if masked:
            w = jnp.where(visible, w, 0.0)
        ws.append(w.astype(BF16))
    pvs = [jnp.dot(ws[h], vbs[h], preferred_element_type=F32) for h in range(n)]
    out = []
    for h in range(n):
        r, acc = carry[h]
        out.append((r + jnp.sum(sp2s[h], axis=-1, keepdims=True), acc + jnp.exp2(-r) * pvs[h]))
    return tuple(out)


def _visit_key_blocks(step, n_blocks, carry):
    def r_min(carry):
        return functools.reduce(jnp.minimum, [jnp.min(r) for r, _ in carry])

    def cond(state):
        i, rmin, _ = state
        return jnp.logical_and(i < n_blocks, rmin < R_STOP)

    def body(state):
        i, _, carry = state
        carry = step(i, carry)
        return i + 1, r_min(carry), carry

    return lax.while_loop(cond, body, (jnp.int32(0), r_min(carry), carry))[2]


def _head_cols(hh):
    return slice(hh * SB_HEADDIM, (hh + 1) * SB_HEADDIM)


def _attn_prompt_kernel(q_ref, k_ref, v_ref, u_ref, o_ref, *, tq, hp):
    qi = pl.program_id(2)
    u = u_ref[...]
    qs = [q_ref[0, :, _head_cols(hh)] for hh in range(hp)]

    def tiles(jb, carry, masked):
        off = pl.multiple_of(jb * tq, tq)
        kbs = [k_ref[0, pl.ds(off, tq), _head_cols(hh)] for hh in range(hp)]
        vbs = [v_ref[0, pl.ds(off, tq), _head_cols(hh)] for hh in range(hp)]
        return _sb_tiles(qs, kbs, vbs, u, carry, masked)

    init = tuple((jnp.zeros((tq, 1), F32), jnp.zeros((tq, SB_HEADDIM), F32)) for _ in range(hp))
    carry = _visit_key_blocks(lambda i, c: tiles(qi - 1 - i, c, False), qi, tiles(qi, init, True))
    for hh in range(hp):
        o_ref[0, :, _head_cols(hh)] = carry[hh][1].astype(o_ref.dtype)


def _attn_prompt(qs, kb, vb, u, b, t, tq, hp):
    q3, k3, v3 = qs.reshape(b, t, D_ATTN), kb.reshape(b, t, D_ATTN), vb.reshape(b, t, D_ATTN)
    w = hp * SB_HEADDIM
    return pl.pallas_call(
        functools.partial(_attn_prompt_kernel, tq=tq, hp=hp),
        grid=(b, SB_HEADS // hp, t // tq),
        in_specs=[
            pl.BlockSpec((1, tq, w), lambda bi, h, qi: (bi, qi, h)),
            pl.BlockSpec((1, t, w), lambda bi, h, qi: (bi, 0, h)),
            pl.BlockSpec((1, t, w), lambda bi, h, qi: (bi, 0, h)),
            pl.BlockSpec((2 * tq, tq), lambda bi, h, qi: (0, 0)),
        ],
        out_specs=pl.BlockSpec((1, tq, w), lambda bi, h, qi: (bi, qi, h)),
        out_shape=jax.ShapeDtypeStruct((b, t, D_ATTN), BF16),
        compiler_params=_cparams(("parallel", "parallel", "arbitrary")),
        name="sb_attn_prompt",
    )(q3, k3, v3, u)


def _attn_sample_kernel(q_ref, kn_ref, vn_ref, kc_ref, vc_ref, un_ref, uc_ref, o_ref, *, tn, tc, n_cache, hp, hb):
    un, uc = un_ref[...], uc_ref[...]
    for h0 in range(0, hb, hp):
        heads = range(h0, h0 + hp)
        qs = [q_ref[0, :, _head_cols(hh)] for hh in heads]
        init = tuple((jnp.zeros((tn, 1), F32), jnp.zeros((tn, SB_HEADDIM), F32)) for _ in heads)
        carry = _sb_tiles(qs, [kn_ref[0, :, _head_cols(hh)] for hh in heads],
                          [vn_ref[0, :, _head_cols(hh)] for hh in heads], un, init, True)

        def body(i, carry, heads=heads, qs=qs):
            off = pl.multiple_of((n_cache - 1 - i) * tc, tc)
            kbs = [kc_ref[0, pl.ds(off, tc), hh, :].astype(BF16) for hh in heads]
            vbs = [vc_ref[0, pl.ds(off, tc), hh, :].astype(BF16) for hh in heads]
            return _sb_tiles(qs, kbs, vbs, uc, carry, False)

        carry = _visit_key_blocks(body, n_cache, carry)
        for hh, (_, acc) in zip(heads, carry):
            o_ref[0, :, _head_cols(hh)] = acc.astype(o_ref.dtype)


def _attn_sample(qs, kb, vb, cache_k, cache_v, un, uc, b, t, past, tc, hp):
    q3, k3, v3 = qs.reshape(b, t, D_ATTN), kb.reshape(b, t, D_ATTN), vb.reshape(b, t, D_ATTN)
    hb = 8
    w = hb * SB_HEADDIM
    new = lambda bi, h: (bi, 0, h)
    cache = lambda bi, h: (bi, 0, h, 0)
    return pl.pallas_call(
        functools.partial(_attn_sample_kernel, tn=t, tc=tc, n_cache=past // tc, hp=hp, hb=hb),
        grid=(b, SB_HEADS // hb),
        in_specs=[
            pl.BlockSpec((1, t, w), new),
            pl.BlockSpec((1, t, w), new),
            pl.BlockSpec((1, t, w), new),
            pl.BlockSpec((1, past, hb, SB_HEADDIM), cache),
            pl.BlockSpec((1, past, hb, SB_HEADDIM), cache),
            pl.BlockSpec((2 * t, t), lambda bi, h: (0, 0)),
            pl.BlockSpec((2 * tc, tc), lambda bi, h: (0, 0)),
        ],
        out_specs=pl.BlockSpec((1, t, w), new),
        out_shape=jax.ShapeDtypeStruct((b, t, D_ATTN), BF16),
        compiler_params=_cparams(("parallel", "parallel")),
        name="sb_attn_sample",
    )(q3, k3, v3, cache_k, cache_v, un, uc)


def _merge_kernel(ya_ref, yb_ref, wa_ref, wb_ref, ga_ref, gb_ref, o_ref):
    a = jnp.dot(ya_ref[...], wa_ref[...], preferred_element_type=F32)
    bb = jnp.dot(yb_ref[...], wb_ref[...], preferred_element_type=F32)
    o_ref[...] = (ga_ref[...] * a + gb_ref[...] * bb).astype(o_ref.dtype)


def _merge(ya, yb, wa, wb, gate, tm, tn):
    m = ya.shape[0]
    nb = D_MODEL // tn
    return pl.pallas_call(
        _merge_kernel,
        grid=(m // tm, nb),
        in_specs=[
            pl.BlockSpec((tm, D_INNER), lambda i, j: (i, 0)),
            pl.BlockSpec((tm, D_ATTN), lambda i, j: (i, 0)),
            pl.BlockSpec((D_INNER, tn), lambda i, j: (0, j)),
            pl.BlockSpec((D_ATTN, tn), lambda i, j: (0, j)),
            pl.BlockSpec((tm, tn), lambda i, j: (i, j)),
            pl.BlockSpec((tm, tn), lambda i, j: (i, j + nb)),
        ],
        out_specs=pl.BlockSpec((tm, tn), lambda i, j: (i, j)),
        out_shape=jax.ShapeDtypeStruct((m, D_MODEL), BF16),
        compiler_params=_cparams(("parallel", "arbitrary")),
        name="branch_merge",
    )(ya, yb, wa, wb, gate, gate)


def _outproj_kernel(m_ref, w_ref, x_ref, g_ref, b_ref, o_ref):
    mix = jnp.dot(m_ref[...], w_ref[...], preferred_element_type=F32)
    o_ref[...] = _layer_norm(ALPHA * x_ref[...] + mix, g_ref[...], b_ref[...])


def _outproj_ln(merged, w_out, x, g, bb, tm):
    m = x.shape[0]
    row = lambda i: (i, 0)
    const = lambda i: (0, 0)
    return pl.pallas_call(
        _outproj_kernel,
        grid=(m // tm,),
        in_specs=[
            pl.BlockSpec((tm, D_MODEL), row),
            pl.BlockSpec((D_MODEL, D_MODEL), const),
            pl.BlockSpec((tm, D_MODEL), row),
            pl.BlockSpec((1, D_MODEL), const),
            pl.BlockSpec((1, D_MODEL), const),
        ],
        out_specs=pl.BlockSpec((tm, D_MODEL), row),
        out_shape=jax.ShapeDtypeStruct((m, D_MODEL), F32),
        compiler_params=_cparams(("parallel",)),
        name="outproj_ln",
    )(merged, w_out, x, g, bb)


def _mlp_kernel(x_ref, wu_ref, bu_ref, wd_ref, bd_ref, g_ref, b_ref, o_ref, xb, acc):
    f = pl.program_id(1)

    @pl.when(f == 0)
    def _():
        xb[...] = x_ref[...].astype(BF16)
        acc[...] = jnp.zeros(acc.shape, F32)

    h = jnp.dot(xb[...], wu_ref[...], preferred_element_type=F32) + bu_ref[...]
    h = jnp.square(jnp.maximum(h, 0.0)).astype(BF16)
    acc[...] += jnp.dot(h, wd_ref[...], preferred_element_type=F32)

    @pl.when(f == pl.num_programs(1) - 1)
    def _():
        o_ref[...] = _layer_norm(ALPHA * x_ref[...] + acc[...] + bd_ref[...], g_ref[...], b_ref[...])


def _mlp_ln(x1, w_up, b_up, w_down, b_down, g, bb, tm, tf):
    m = x1.shape[0]
    row = lambda i, f: (i, 0)
    const = lambda i, f: (0, 0)
    return pl.pallas_call(
        _mlp_kernel,
        grid=(m // tm, D_FF // tf),
        in_specs=[
            pl.BlockSpec((tm, D_MODEL), row),
            pl.BlockSpec((D_MODEL, tf), lambda i, f: (0, f)),
            pl.BlockSpec((1, tf), lambda i, f: (0, f)),
            pl.BlockSpec((tf, D_MODEL), lambda i, f: (f, 0)),
            pl.BlockSpec((1, D_MODEL), const),
            pl.BlockSpec((1, D_MODEL), const),
            pl.BlockSpec((1, D_MODEL), const),
        ],
        out_specs=pl.BlockSpec((tm, D_MODEL), row),
        out_shape=jax.ShapeDtypeStruct((m, D_MODEL), F32),
        scratch_shapes=[pltpu.VMEM((tm, D_MODEL), BF16), pltpu.VMEM((tm, D_MODEL), F32)],
        compiler_params=_cparams(("parallel", "arbitrary")),
        name="mlp_ln",
    )(x1, w_up, b_up, w_down, b_down, g, bb)


def _trunk(x, conv_prev, h0, k_past, v_past, p):
    b, t, _ = x.shape
    m = b * t
    x2 = x.reshape(m, D_MODEL)
    tm = min(m, 1024)
    z, xbc, q, k, v, gate, dtr, kb, vb = _in_proj(x2.astype(BF16), p["w_a"], p["w_b"], p["w_dt"], p["b_gate"], tm, 512)

    if conv_prev is None:
        conv_prev8, h0t = None, None
    else:
        conv_prev8 = jnp.pad(conv_prev, ((0, 0), (8 - (CONV_W - 1), 0), (0, 0)))
        h0t = h0.reshape(b, SSD_GROUPS, GROUP_W, D_STATE).transpose(0, 1, 3, 2)
    y_ssd, ht = _ssd(xbc, dtr, z, conv_prev8, h0t, p["conv_w"], p["conv_b"], p["dt_bias"], p["a_exp"],
                     p["dskip_exp"], p["norm_w"], p["e_mat"], p["tri"], b, t)
    h_new = ht.transpose(0, 1, 3, 2).reshape(b, SSD_HEADS, SSD_HEADDIM, D_STATE)
    conv_new = xbc.reshape(b, t, CONV_DIM)[:, t - (CONV_W - 1):, :]

    if k_past is None:
        tq = 256
        y_sb = _attn_prompt(q, kb, vb, p["u"][tq], b, t, tq, ATTN_HEADS_PER_STEP)
    else:
        tc = 256
        y_sb = _attn_sample(q, kb, vb, k_past, v_past, p["u"][t], p["u"][tc], b, t, k_past.shape[1], tc,
                            ATTN_HEADS_PER_STEP)

    merged = _merge(y_ssd.reshape(m, D_INNER), y_sb.reshape(m, D_ATTN), p["w_br_ssd"], p["w_br_attn"], gate,
                    min(m, 512), 512)
    x1 = _outproj_ln(merged, p["w_out"], x2, p["ln1_g"], p["ln1_b"], min(m, 512))
    x_out = _mlp_ln(x1, p["w_up"], p["b_up"], p["w_down"], p["b_down"], p["ln2_g"], p["ln2_b"], min(m, 512), 1024)
    kv_shape = (b, t, SB_HEADS, SB_HEADDIM)
    return x_out.reshape(b, t, D_MODEL), conv_new, h_new, k.reshape(kv_shape), v.reshape(kv_shape)


def _suffix_sum_matrix(n):
    u = jnp.tril(jnp.ones((n, n), F32)).astype(BF16)
    return jnp.concatenate([u, u], axis=0)


def _layer_params(l, w_in, b_gate, conv_w, conv_b, dt_bias, a_log, d_skip, ssm_norm_w, w_br_ssd, w_br_attn, w_out,
                  ln1_g, ln1_b, w_up, b_up, w_down, b_down, ln2_g, ln2_b):
    w = w_in[l]
    o_dt = D_INNER + CONV_DIM
    o_q = o_dt + SSD_HEADS
    w_dt = jnp.pad(w[:, o_dt:o_q], ((0, 0), (0, 128 - SSD_HEADS))).astype(BF16)
    row = lambda a: a[l].reshape(1, -1)
    head_of_lane = jnp.arange(D_INNER) // SSD_HEADDIM
    return {
        "w_a": w[:, :o_dt].astype(BF16),
        "w_b": w[:, o_q:].astype(BF16),
        "w_dt": w_dt,
        "b_gate": row(b_gate),
        "conv_w": conv_w[l],
        "conv_b": row(conv_b),
        "dt_bias": row(dt_bias),
        "a_exp": jnp.repeat(-jnp.exp(a_log[l].astype(F32)), SSD_HEADDIM).reshape(1, D_INNER),
        "dskip_exp": jnp.repeat(d_skip[l], SSD_HEADDIM).reshape(1, D_INNER),
        "norm_w": row(ssm_norm_w),
        "e_mat": jnp.tile(jnp.arange(SSD_HEADS)[:, None] == head_of_lane[None, :], (2, 1)).astype(BF16),
        "tri": jnp.tile(jnp.tril(jnp.ones((CHUNK, CHUNK), F32)), (1, 3)).astype(BF16),
        "u": {n: _suffix_sum_matrix(n) for n in (64, 256)},
        "w_br_ssd": w_br_ssd[l].astype(BF16),
        "w_br_attn": w_br_attn[l].astype(BF16),
        "w_out": w_out[l].astype(BF16),
        "ln1_g": row(ln1_g),
        "ln1_b": row(ln1_b),
        "w_up": w_up[l].astype(BF16),
        "b_up": row(b_up),
        "w_down": w_down[l].astype(BF16),
        "b_down": row(b_down),
        "ln2_g": row(ln2_g),
        "ln2_b": row(ln2_b),
    }


def kernel(x_prompt, x_sample, cache_conv, state_ssm, cache_k, cache_v, w_in, b_gate, conv_w, conv_b, dt_bias, a_log, d_skip, ssm_norm_w, w_br_ssd, w_br_attn, w_out, ln1_g, ln1_b, w_up, b_up, w_down, b_down, ln2_g, ln2_b):
    hp, hs = x_prompt, x_sample
    outs_p, outs_s = [], []
    for l in range(DEPTH):
        p = _layer_params(l, w_in, b_gate, conv_w, conv_b, dt_bias, a_log, d_skip, ssm_norm_w, w_br_ssd, w_br_attn,
                          w_out, ln1_g, ln1_b, w_up, b_up, w_down, b_down, ln2_g, ln2_b)
        hp, c1, s1, k1, v1 = _trunk(hp, None, None, None, None, p)
        outs_p.append((c1, s1, k1, v1))
        hs, c2, s2, k2, v2 = _trunk(hs, cache_conv[l], state_ssm[l], cache_k[l], cache_v[l], p)
        outs_s.append((c2, s2, k2, v2))
    stack = lambda outs, i: jnp.stack([o[i] for o in outs])
    return (hp, hs, stack(outs_p, 0), stack(outs_p, 1), stack(outs_p, 2), stack(outs_p, 3),
            stack(outs_s, 0), stack(outs_s, 1), stack(outs_s, 2), stack(outs_s, 3))
```

```python
import functools
import math

import jax
import jax.numpy as jnp
from jax import lax
from jax.experimental import pallas as pl
from jax.experimental.pallas import tpu as pltpu

D_MODEL = 2048
D_INNER = 2 * D_MODEL
SSD_HEADDIM = 64
SSD_HEADS = D_INNER // SSD_HEADDIM
SSD_GROUPS = 8
HEADS_PER_GROUP = SSD_HEADS // SSD_GROUPS
GROUP_W = D_INNER // SSD_GROUPS
D_STATE = 128
CONV_W = 4
CONV_DIM = D_INNER + 2 * SSD_GROUPS * D_STATE
SB_HEADDIM = 128
SB_HEADS = D_MODEL // SB_HEADDIM
D_ATTN = SB_HEADS * SB_HEADDIM
SB_SCALE = 1.0 / math.sqrt(SB_HEADDIM)
D_FF = 4 * D_MODEL
DEPTH = 1
ALPHA = (2.0 * DEPTH) ** 0.25
LOG2E = math.log2(math.e)
SP2_CLAMP = 126.0
R_STOP = 160.0
LN_EPS = 1e-5
RMS_EPS = 1e-5
CHUNK = 64
ATTN_HEADS_PER_STEP = 4
SSD_GROUP_UNROLL = 4

V7X_VMEM_BYTES = 64 * 1024 * 1024
VMEM_LIMIT = 56 * 1024 * 1024

F32 = jnp.float32
BF16 = jnp.bfloat16


def _cparams(sem):
    return pltpu.CompilerParams(dimension_semantics=sem, vmem_limit_bytes=VMEM_LIMIT)


def _sigmoid(x):
    return 1.0 / (1.0 + jnp.exp(-x))


def _softplus(x):
    return jnp.maximum(x, 0.0) + jnp.log1p(jnp.exp(-jnp.abs(x)))


def _split_bf16(x):
    hi = x.astype(BF16)
    lo = (x - hi.astype(F32)).astype(BF16)
    return hi, lo


def _split3_bf16(x):
    hi = x.astype(BF16)
    r1 = x - hi.astype(F32)
    mid = r1.astype(BF16)
    lo = (r1 - mid.astype(F32)).astype(BF16)
    return hi, mid, lo


def _layer_norm(x, g, b):
    mu = jnp.mean(x, axis=-1, keepdims=True)
    xc = x - mu
    var = jnp.mean(xc * xc, axis=-1, keepdims=True)
    return xc * lax.rsqrt(var + LN_EPS) * g + b


P1_XBC, P1_ZS, P1_GATE, P1_W = 0, 8192, 12288, 16384
P2_Q, P2_K, P2_V, P2_W = 0, D_ATTN, 2 * D_ATTN, 3 * D_ATTN


def _inproj_kernel(*refs, ends, seq_tiles):
    if seq_tiles is None:
        x_ref, wa_ref, wb_ref, wdt_ref, bg_ref, p1_ref, k_ref, v_ref, p2_ref, dt_ref = refs
    else:
        (x_ref, wa_ref, wb_ref, wdt_ref, bg_ref, cw_ref, cb_ref,
         p1_ref, k_ref, v_ref, p2_ref, dt_ref, tail_ref, hist) = refs
    i, j = pl.program_id(0), pl.program_id(1)
    ez, exbc, eq, ek, ev, _ = ends
    tm = x_ref.shape[0]

    def proj(w_ref):
        return jnp.dot(x_ref[...], w_ref[...], preferred_element_type=F32)

    @pl.when(j == 0)
    def _():
        dt_ref[...] = jnp.dot(x_ref[...], wdt_ref[...], preferred_element_type=F32)

    @pl.when(j < ez)
    def _():
        acc = proj(wa_ref)
        p1_ref[...] = acc * _sigmoid(acc)

    @pl.when((j >= ez) & (j < exbc))
    def _():
        raw = proj(wa_ref)
        if seq_tiles is None:
            p1_ref[...] = raw
        else:
            jx = j - ez
            prev = jnp.where(i % seq_tiles == 0, 0.0, hist[jx])
            hist[jx] = raw[tm - 8:, :]
            tail_ref[0] = raw[tm - 8:, :]
            ext = jnp.concatenate([prev, raw], axis=0)
            acc = cb_ref[...] + cw_ref[CONV_W - 1:CONV_W, :] * raw
            for t in range(CONV_W - 1):
                acc = acc + cw_ref[t:t + 1, :] * ext[5 + t:5 + t + tm, :]
            p1_ref[...] = acc * _sigmoid(acc)

    @pl.when((j >= exbc) & (j < eq))
    def _():
        p2_ref[...] = (proj(wb_ref) * (SB_SCALE * LOG2E)).astype(BF16)

    @pl.when((j >= eq) & (j < ek))
    def _():
        acc = proj(wb_ref)
        k_ref[...] = acc
        p2_ref[...] = acc.astype(BF16)

    @pl.when((j >= ek) & (j < ev))
    def _():
        acc = proj(wb_ref)
        v_ref[...] = acc
        p2_ref[...] = acc.astype(BF16)

    @pl.when(j >= ev)
    def _():
        p1_ref[...] = _sigmoid(proj(wb_ref) + bg_ref[...])


def _in_proj(xb, w_a, w_b, w_dt, b_gate, conv_w, conv_b, tm, tn, seq_tiles):
    m = xb.shape[0]
    widths = (D_INNER, CONV_DIM, D_ATTN, D_ATTN, D_ATTN, 2 * D_MODEL)
    ends, e = [], 0
    for w in widths:
        assert w % tn == 0
        e += w // tn
        ends.append(e)
    ez, exbc, eq, ek, ev, eg = ends
    nxbc = CONV_DIM // tn

    def p1_map(i, j):
        blk = jnp.where(j < ez, j + P1_ZS // tn,
                        jnp.where(j < exbc, j - ez, jnp.where(j < ev, nxbc - 1, j - ev + P1_GATE // tn)))
        return (i, blk)

    xbc_map = lambda i, j: (0, jnp.clip(j - ez, 0, nxbc - 1))
    in_specs = [
        pl.BlockSpec((tm, D_MODEL), lambda i, j: (i, 0), pipeline_mode=pl.Buffered(1)),
        pl.BlockSpec((D_MODEL, tn), lambda i, j: (0, jnp.minimum(j, exbc - 1))),
        pl.BlockSpec((D_MODEL, tn), lambda i, j: (0, jnp.maximum(j - exbc, 0))),
        pl.BlockSpec((D_MODEL, 128), lambda i, j: (0, 0)),
        pl.BlockSpec((1, tn), lambda i, j: (0, jnp.clip(j - ev, 0, eg - ev - 1))),
    ]
    args = [xb, w_a, w_b, w_dt, b_gate]
    out_shapes = [
        jax.ShapeDtypeStruct((m, P1_W), F32),
        jax.ShapeDtypeStruct((m, D_ATTN), F32),
        jax.ShapeDtypeStruct((m, D_ATTN), F32),
        jax.ShapeDtypeStruct((m, P2_W), BF16),
        jax.ShapeDtypeStruct((m, 128), F32),
    ]
    out_specs = [
        pl.BlockSpec((tm, tn), p1_map),
        pl.BlockSpec((tm, tn), lambda i, j: (i, jnp.clip(j - eq, 0, ek - eq - 1))),
        pl.BlockSpec((tm, tn), lambda i, j: (i, jnp.clip(j - ek, 0, ev - ek - 1))),
        pl.BlockSpec((tm, tn), lambda i, j: (i, jnp.clip(j - exbc, 0, ev - exbc - 1))),
        pl.BlockSpec((tm, 128), lambda i, j: (i, 0)),
    ]
    scratch = []
    if seq_tiles is not None:
        in_specs += [pl.BlockSpec((CONV_W, tn), xbc_map), pl.BlockSpec((1, tn), xbc_map)]
        args += [conv_w, conv_b]
        out_shapes.append(jax.ShapeDtypeStruct((m // tm, 8, CONV_DIM), F32))
        out_specs.append(pl.BlockSpec((1, 8, tn), lambda i, j: (i, 0, jnp.clip(j - ez, 0, nxbc - 1))))
        scratch.append(pltpu.VMEM((nxbc, 8, tn), F32))
    return pl.pallas_call(
        functools.partial(_inproj_kernel, ends=tuple(ends), seq_tiles=seq_tiles),
        grid=(m // tm, eg),
        in_specs=in_specs,
        out_specs=out_specs,
        out_shape=out_shapes,
        scratch_shapes=scratch,
        compiler_params=_cparams(("arbitrary", "arbitrary")),
        name="in_proj",
    )(*args)


def _ssd_kernel(*refs, has_history):
    if has_history:
        (xbc_ref, dtr_ref, z_ref, cprev_ref, h0_ref, convw_ref, convb_ref, dtb_ref, aexp_ref, dskip_ref,
         nw_ref, e_ref, tri_ref, y_ref, ht_ref, xpad, state) = refs
    else:
        (xbc_ref, dtr_ref, z_ref, dtb_ref, aexp_ref, dskip_ref,
         nw_ref, e_ref, tri_ref, y_ref, ht_ref, state) = refs
    c = pl.program_id(1)
    nc = pl.num_programs(1)
    L = CHUNK

    @pl.when(c == 0)
    def _():
        if has_history:
            xpad[0:8, :] = cprev_ref[0]
            state[...] = h0_ref[0]
        else:
            state[...] = jnp.zeros(state.shape, F32)

    if has_history:
        xpad[8:8 + L, :] = xbc_ref[0]

    dt = _softplus(dtr_ref[0][:, :SSD_HEADS] + dtb_ref[...])
    dt_hilo = jnp.concatenate(_split_bf16(dt), axis=1)
    tri3 = tri_ref[...]
    row = lax.broadcasted_iota(jnp.int32, (L, GROUP_W), 0)
    lane = lax.broadcasted_iota(jnp.int32, (L, GROUP_W), 1)
    pos_in_head = lane & (SSD_HEADDIM - 1)
    diag_sel = row == pos_in_head
    causal = row >= pos_in_head
    lane128 = lax.broadcasted_iota(jnp.int32, (L, 128), 1)
    lo_half = lane128 < SSD_HEADDIM

    def conv_cols(start, width):
        sl = pl.ds(start, width)
        if not has_history:
            return xbc_ref[0, :, sl]
        acc = convb_ref[:, sl] + convw_ref[CONV_W - 1:CONV_W, sl] * xpad[8:8 + L, sl]
        for t in range(CONV_W - 1):
            acc = acc + convw_ref[t:t + 1, sl] * xpad[5 + t:5 + t + L, sl]
        return acc * _sigmoid(acc)

    def groups_body(it, carry):
        gs = [it * SSD_GROUP_UNROLL + k for k in range(SSD_GROUP_UNROLL)]
        xoffs = [pl.multiple_of(g * GROUP_W, GROUP_W) for g in gs]
        xss, bgs, cgs = [], [], []
        for g, xoff in zip(gs, xoffs):
            boff = pl.multiple_of(D_INNER + g * D_STATE, D_STATE)
            coff = pl.multiple_of(D_INNER + SSD_GROUPS * D_STATE + g * D_STATE, D_STATE)
            xss.append(conv_cols(xoff, GROUP_W))
            bgs.append(conv_cols(boff, D_STATE).astype(BF16))
            cgs.append(conv_cols(coff, D_STATE).astype(BF16))

        dt_exps = [jnp.dot(dt_hilo, e_ref[:, pl.ds(xoff, GROUP_W)], preferred_element_type=F32) for xoff in xoffs]
        da_parts = [jnp.concatenate(_split3_bf16(dt_exp * aexp_ref[:, pl.ds(xoff, GROUP_W)]), axis=0)
                    for dt_exp, xoff in zip(dt_exps, xoffs)]
        acss = [jnp.dot(tri3, parts, preferred_element_type=F32) for parts in da_parts]
        cb2s = [lax.dot_general(cg, jnp.concatenate([bg, bg], axis=0), (((1,), (1,)), ((), ())),
                                preferred_element_type=F32) for cg, bg in zip(cgs, bgs)]

        zero = jnp.zeros((L, 128), BF16)
        eacss, a_lasts, xdds, mmats, bds = [], [], [], [], []
        for xs, dt_exp, acs, cb2 in zip(xss, dt_exps, acss, cb2s):
            a_last = acs[L - 1:L, :]
            a_lasts.append(a_last)
            eacss.append(jnp.exp(acs))
            xd = xs * dt_exp
            xdds.append((xd * jnp.exp(a_last - acs)).astype(BF16))
            xdb = xd.astype(BF16)
            rowform = jnp.sum(jnp.where(diag_sel, acs, 0.0), axis=0, keepdims=True)
            lmat = jnp.where(causal, jnp.exp(jnp.minimum(acs - rowform, 0.0)), 0.0)
            mmats.append((lmat * jnp.concatenate([cb2] * (GROUP_W // 128), axis=1)).astype(BF16))
            bds.append([jnp.concatenate([jnp.where(lo_half, xdb[:, 128 * p:128 * (p + 1)], zero),
                                         jnp.where(lo_half, zero, xdb[:, 128 * p:128 * (p + 1)])], axis=0)
                        for p in range(GROUP_W // 128)])

        sts = [state[g] for g in gs]
        ydiags = [jnp.concatenate([jnp.dot(mmat[:, 128 * p:128 * (p + 1)], bd[p], preferred_element_type=F32)
                                   for p in range(GROUP_W // 128)], axis=1) for mmat, bd in zip(mmats, bds)]
        yoffs = [jnp.dot(cg, st.astype(BF16), preferred_element_type=F32) for cg, st in zip(cgs, sts)]
        snews = [lax.dot_general(bg, xdd, (((0,), (0,)), ((), ())), preferred_element_type=F32)
                 for bg, xdd in zip(bgs, xdds)]

        for k, (g, xoff) in enumerate(zip(gs, xoffs)):
            cols = pl.ds(xoff, GROUP_W)
            y = ydiags[k] + yoffs[k] * eacss[k] + dskip_ref[:, cols] * xss[k]
            state[g] = sts[k] * jnp.exp(a_lasts[k]) + snews[k]
            yg = y * z_ref[0, :, cols]
            ms = jnp.mean(yg * yg, axis=-1, keepdims=True)
            y_ref[0, :, cols] = (yg * lax.rsqrt(ms + RMS_EPS) * nw_ref[:, cols]).astype(y_ref.dtype)
        return carry

    lax.fori_loop(0, SSD_GROUPS // SSD_GROUP_UNROLL, groups_body, 0)
    if has_history:
        xpad[0:8, :] = xpad[L:L + 8, :]

    @pl.when(c == nc - 1)
    def _():
        ht_ref[0] = state[...]


def _ssd(p1, dtr, conv_prev8, h0t, conv_w, conv_b, dt_bias, a_exp, dskip_exp, norm_w, e_mat, tri, b, t):
    nc = t // CHUNK
    has_history = conv_prev8 is not None
    tok = lambda bi, c: (bi, c, 0)
    const2 = lambda bi, c: (0, 0)
    p1 = p1.reshape(b, t, P1_W)
    in_specs = [
        pl.BlockSpec((1, CHUNK, CONV_DIM), lambda bi, c: (bi, c, P1_XBC // CONV_DIM)),
        pl.BlockSpec((1, CHUNK, 128), tok),
        pl.BlockSpec((1, CHUNK, D_INNER), lambda bi, c: (bi, c, P1_ZS // D_INNER)),
    ]
    args = [p1, dtr.reshape(b, t, 128), p1]
    if has_history:
        in_specs += [
            pl.BlockSpec((1, 8, CONV_DIM), lambda bi, c: (bi, 0, 0)),
            pl.BlockSpec((1, SSD_GROUPS, D_STATE, GROUP_W), lambda bi, c: (bi, 0, 0, 0)),
            pl.BlockSpec((CONV_W, CONV_DIM), const2),
            pl.BlockSpec((1, CONV_DIM), const2),
        ]
        args += [conv_prev8, h0t, conv_w, conv_b]
    in_specs += [
        pl.BlockSpec((1, SSD_HEADS), const2),
        pl.BlockSpec((1, D_INNER), const2),
        pl.BlockSpec((1, D_INNER), const2),
        pl.BlockSpec((1, D_INNER), const2),
        pl.BlockSpec((2 * SSD_HEADS, D_INNER), const2),
        pl.BlockSpec((CHUNK, 3 * CHUNK), const2),
    ]
    args += [dt_bias, a_exp, dskip_exp, norm_w, e_mat, tri]
    scratch = [pltpu.VMEM((SSD_GROUPS, D_STATE, GROUP_W), F32)]
    if has_history:
        scratch.insert(0, pltpu.VMEM((8 + CHUNK, CONV_DIM), F32))
    return pl.pallas_call(
        functools.partial(_ssd_kernel, has_history=has_history),
        grid=(b, nc),
        in_specs=in_specs,
        out_specs=[
            pl.BlockSpec((1, CHUNK, D_INNER), tok),
            pl.BlockSpec((1, SSD_GROUPS, D_STATE, GROUP_W), lambda bi, c: (bi, 0, 0, 0)),
        ],
        out_shape=(
            jax.ShapeDtypeStruct((b, t, D_INNER), BF16),
            jax.ShapeDtypeStruct((b, SSD_GROUPS, D_STATE, GROUP_W), F32),
        ),
        scratch_shapes=scratch,
        compiler_params=_cparams(("parallel", "arbitrary")),
        name="conv_ssd",
    )(*args)


def _sb_tiles(qs, kbs, vbs, u, carry, masked):
    n = len(qs)
    ts = [lax.dot_general(qs[h], kbs[h], (((1,), (1,)), ((), ())), preferred_element_type=F32) for h in range(n)]
    if masked:
        tq, tk = ts[0].shape
        visible = lax.broadcasted_iota(jnp.int32, (tq, tk), 1) < lax.broadcasted_iota(jnp.int32, (tq, tk), 0)
    sp2s, hilos = [], []
    for t in ts:
        sp2 = jnp.maximum(t, jnp.log(1.0 + jnp.exp2(jnp.minimum(t, SP2_CLAMP))) * LOG2E)
        if masked:
            sp2 = jnp.where(visible, sp2, 0.0)
        sp2s.append(sp2)
        hilos.append(jnp.concatenate(_split_bf16(sp2), axis=1))
    sums = [jnp.dot(hl, u, preferred_element_type=F32) for hl in hilos]
    ws = []
    for t, s in zip(ts, sums):
        w = jnp.exp2(t - s)
        if masked:
            w = jnp.where(visible, w, 0.0)
        ws.append(w.astype(BF16))
    pvs = [jnp.dot(ws[h], vbs[h], preferred_element_type=F32) for h in range(n)]
    out = []
    for h in range(n):
        r, acc = carry[h]
        out.append((r + jnp.sum(sp2s[h], axis=-1, keepdims=True), acc + jnp.exp2(-r) * pvs[h]))
    return tuple(out)


def _visit_key_blocks(step, n_blocks, carry):
    def r_min(carry):
        return functools.reduce(jnp.minimum, [jnp.min(r) for r, _ in carry])

    def cond(state):
        i, rmin, _ = state
        return jnp.logical_and(i < n_blocks, rmin < R_STOP)

    def body(state):
        i, _, carry = state
        carry = step(i, carry)
        return i + 1, r_min(carry), carry

    return lax.while_loop(cond, body, (jnp.int32(0), r_min(carry), carry))[2]


def _head_cols(hh):
    return slice(hh * SB_HEADDIM, (hh + 1) * SB_HEADDIM)


def _attn_prompt_kernel(q_ref, k_ref, v_ref, u_ref, o_ref, *, tq, hp):
    qi = pl.program_id(2)
    u = u_ref[...]
    qs = [q_ref[0, :, _head_cols(hh)] for hh in range(hp)]

    def tiles(jb, carry, masked):
        off = pl.multiple_of(jb * tq, tq)
        kbs = [k_ref[0, pl.ds(off, tq), _head_cols(hh)] for hh in range(hp)]
        vbs = [v_ref[0, pl.ds(off, tq), _head_cols(hh)] for hh in range(hp)]
        return _sb_tiles(qs, kbs, vbs, u, carry, masked)

    init = tuple((jnp.zeros((tq, 1), F32), jnp.zeros((tq, SB_HEADDIM), F32)) for _ in range(hp))
    carry = _visit_key_blocks(lambda i, c: tiles(qi - 1 - i, c, False), qi, tiles(qi, init, True))
    for hh in range(hp):
        o_ref[0, :, _head_cols(hh)] = carry[hh][1].astype(o_ref.dtype)


def _attn_prompt(p2, u, b, t, tq, hp):
    q3 = k3 = v3 = p2.reshape(b, t, P2_W)
    w = hp * SB_HEADDIM
    return pl.pallas_call(
        functools.partial(_attn_prompt_kernel, tq=tq, hp=hp),
        grid=(b, SB_HEADS // hp, t // tq),
        in_specs=[
            pl.BlockSpec((1, tq, w), lambda bi, h, qi: (bi, qi, P2_Q // w + h)),
            pl.BlockSpec((1, t, w), lambda bi, h, qi: (bi, 0, P2_K // w + h)),
            pl.BlockSpec((1, t, w), lambda bi, h, qi: (bi, 0, P2_V // w + h)),
            pl.BlockSpec((2 * tq, tq), lambda bi, h, qi: (0, 0)),
        ],
        out_specs=pl.BlockSpec((1, tq, w), lambda bi, h, qi: (bi, qi, h)),
        out_shape=jax.ShapeDtypeStruct((b, t, D_ATTN), BF16),
        compiler_params=_cparams(("parallel", "parallel", "arbitrary")),
        name="sb_attn_prompt",
    )(q3, k3, v3, u)


def _attn_sample_kernel(q_ref, kn_ref, vn_ref, kc_ref, vc_ref, un_ref, uc_ref, o_ref, *, tn, tc, n_cache, hp, hb):
    un, uc = un_ref[...], uc_ref[...]
    for h0 in range(0, hb, hp):
        heads = range(h0, h0 + hp)
        qs = [q_ref[0, :, _head_cols(hh)] for hh in heads]
        init = tuple((jnp.zeros((tn, 1), F32), jnp.zeros((tn, SB_HEADDIM), F32)) for _ in heads)
        carry = _sb_tiles(qs, [kn_ref[0, :, _head_cols(hh)] for hh in heads],
                          [vn_ref[0, :, _head_cols(hh)] for hh in heads], un, init, True)

        def body(i, carry, heads=heads, qs=qs):
            off = pl.multiple_of((n_cache - 1 - i) * tc, tc)
            kbs = [kc_ref[0, pl.ds(off, tc), hh, :].astype(BF16) for hh in heads]
            vbs = [vc_ref[0, pl.ds(off, tc), hh, :].astype(BF16) for hh in heads]
            return _sb_tiles(qs, kbs, vbs, uc, carry, False)

        carry = _visit_key_blocks(body, n_cache, carry)
        for hh, (_, acc) in zip(heads, carry):
            o_ref[0, :, _head_cols(hh)] = acc.astype(o_ref.dtype)


def _attn_sample(p2, cache_k, cache_v, un, uc, b, t, past, tc, hp):
    q3 = k3 = v3 = p2.reshape(b, t, P2_W)
    hb = 8
    w = hb * SB_HEADDIM
    new = lambda bi, h: (bi, 0, h)
    cache = lambda bi, h: (bi, 0, h, 0)
    return pl.pallas_call(
        functools.partial(_attn_sample_kernel, tn=t, tc=tc, n_cache=past // tc, hp=hp, hb=hb),
        grid=(b, SB_HEADS // hb),
        in_specs=[
            pl.BlockSpec((1, t, w), lambda bi, h: (bi, 0, P2_Q // w + h)),
            pl.BlockSpec((1, t, w), lambda bi, h: (bi, 0, P2_K // w + h)),
            pl.BlockSpec((1, t, w), lambda bi, h: (bi, 0, P2_V // w + h)),
            pl.BlockSpec((1, past, hb, SB_HEADDIM), cache),
            pl.BlockSpec((1, past, hb, SB_HEADDIM), cache),
            pl.BlockSpec((2 * t, t), lambda bi, h: (0, 0)),
            pl.BlockSpec((2 * tc, tc), lambda bi, h: (0, 0)),
        ],
        out_specs=pl.BlockSpec((1, t, w), new),
        out_shape=jax.ShapeDtypeStruct((b, t, D_ATTN), BF16),
        compiler_params=_cparams(("parallel", "parallel")),
        name="sb_attn_sample",
    )(q3, k3, v3, cache_k, cache_v, un, uc)


def _merge_kernel(ya_ref, yb_ref, wa_ref, wb_ref, ga_ref, gb_ref, o_ref):
    a = jnp.dot(ya_ref[...], wa_ref[...], preferred_element_type=F32)
    bb = jnp.dot(yb_ref[...], wb_ref[...], preferred_element_type=F32)
    o_ref[...] = (ga_ref[...] * a + gb_ref[...] * bb).astype(o_ref.dtype)


def _merge(ya, yb, wa, wb, p1, tm, tn):
    m = ya.shape[0]
    nb = D_MODEL // tn
    g0 = P1_GATE // tn
    return pl.pallas_call(
        _merge_kernel,
        grid=(m // tm, nb),
        in_specs=[
            pl.BlockSpec((tm, D_INNER), lambda i, j: (i, 0)),
            pl.BlockSpec((tm, D_ATTN), lambda i, j: (i, 0)),
            pl.BlockSpec((D_INNER, tn), lambda i, j: (0, j)),
            pl.BlockSpec((D_ATTN, tn), lambda i, j: (0, j)),
            pl.BlockSpec((tm, tn), lambda i, j: (i, g0 + j)),
            pl.BlockSpec((tm, tn), lambda i, j: (i, g0 + nb + j)),
        ],
        out_specs=pl.BlockSpec((tm, tn), lambda i, j: (i, j)),
        out_shape=jax.ShapeDtypeStruct((m, D_MODEL), BF16),
        compiler_params=_cparams(("parallel", "arbitrary")),
        name="branch_merge",
    )(ya, yb, wa, wb, p1, p1)


def _outproj_kernel(m_ref, w_ref, x_ref, g_ref, b_ref, o_ref):
    mix = jnp.dot(m_ref[...], w_ref[...], preferred_element_type=F32)
    o_ref[...] = _layer_norm(ALPHA * x_ref[...] + mix, g_ref[...], b_ref[...])


def _outproj_ln(merged, w_out, x, g, bb, tm):
    m = x.shape[0]
    row = lambda i: (i, 0)
    const = lambda i: (0, 0)
    return pl.pallas_call(
        _outproj_kernel,
        grid=(m // tm,),
        in_specs=[
            pl.BlockSpec((tm, D_MODEL), row),
            pl.BlockSpec((D_MODEL, D_MODEL), const),
            pl.BlockSpec((tm, D_MODEL), row),
            pl.BlockSpec((1, D_MODEL), const),
            pl.BlockSpec((1, D_MODEL), const),
        ],
        out_specs=pl.BlockSpec((tm, D_MODEL), row),
        out_shape=jax.ShapeDtypeStruct((m, D_MODEL), F32),
        compiler_params=_cparams(("parallel",)),
        name="outproj_ln",
    )(merged, w_out, x, g, bb)


def _mlp_kernel(x_ref, wu_ref, bu_ref, wd_ref, bd_ref, g_ref, b_ref, o_ref, xb, acc):
    f = pl.program_id(1)

    @pl.when(f == 0)
    def _():
        xb[...] = x_ref[...].astype(BF16)
        acc[...] = jnp.zeros(acc.shape, F32)

    h = jnp.dot(xb[...], wu_ref[...], preferred_element_type=F32) + bu_ref[...]
    h = jnp.square(jnp.maximum(h, 0.0)).astype(BF16)
    acc[...] += jnp.dot(h, wd_ref[...], preferred_element_type=F32)

    @pl.when(f == pl.num_programs(1) - 1)
    def _():
        o_ref[...] = _layer_norm(ALPHA * x_ref[...] + acc[...] + bd_ref[...], g_ref[...], b_ref[...])


def _mlp_ln(x1, w_up, b_up, w_down, b_down, g, bb, tm, tf):
    m = x1.shape[0]
    row = lambda i, f: (i, 0)
    const = lambda i, f: (0, 0)
    return pl.pallas_call(
        _mlp_kernel,
        grid=(m // tm, D_FF // tf),
        in_specs=[
            pl.BlockSpec((tm, D_MODEL), row),
            pl.BlockSpec((D_MODEL, tf), lambda i, f: (0, f)),
            pl.BlockSpec((1, tf), lambda i, f: (0, f)),
            pl.BlockSpec((tf, D_MODEL), lambda i, f: (f, 0)),
            pl.BlockSpec((1, D_MODEL), const),
            pl.BlockSpec((1, D_MODEL), const),
            pl.BlockSpec((1, D_MODEL), const),
        ],
        out_specs=pl.BlockSpec((tm, D_MODEL), row),
        out_shape=jax.ShapeDtypeStruct((m, D_MODEL), F32),
        scratch_shapes=[pltpu.VMEM((tm, D_MODEL), BF16), pltpu.VMEM((tm, D_MODEL), F32)],
        compiler_params=_cparams(("parallel", "arbitrary")),
        name="mlp_ln",
    )(x1, w_up, b_up, w_down, b_down, g, bb)


def _trunk(x, conv_prev, h0, k_past, v_past, p):
    b, t, _ = x.shape
    m = b * t
    x2 = x.reshape(m, D_MODEL)
    tm = min(m, 1024)
    keep = CONV_W - 1
    if conv_prev is None:
        assert t % tm == 0
        p1, k, v, p2, dtr, tails = _in_proj(x2.astype(BF16), p["w_a"], p["w_b"], p["w_dt"], p["b_gate"],
                                            p["conv_w"], p["conv_b"], tm, 1024, t // tm)
        conv_prev8, h0t = None, None
        conv_new = tails.reshape(b, t // tm, 8, CONV_DIM)[:, -1, 8 - keep:, :]
    else:
        p1, k, v, p2, dtr = _in_proj(x2.astype(BF16), p["w_a"], p["w_b"], p["w_dt"], p["b_gate"],
                                     None, None, tm, 1024, None)
        conv_prev8 = jnp.pad(conv_prev, ((0, 0), (8 - keep, 0), (0, 0)))
        h0t = h0.reshape(b, SSD_GROUPS, GROUP_W, D_STATE).transpose(0, 1, 3, 2)
        conv_new = p1.reshape(b, t, P1_W)[:, t - keep:, P1_XBC:P1_XBC + CONV_DIM]
    y_ssd, ht = _ssd(p1, dtr, conv_prev8, h0t, p["conv_w"], p["conv_b"], p["dt_bias"], p["a_exp"],
                     p["dskip_exp"], p["norm_w"], p["e_mat"], p["tri"], b, t)
    h_new = ht.transpose(0, 1, 3, 2).reshape(b, SSD_HEADS, SSD_HEADDIM, D_STATE)

    if k_past is None:
        tq = 256
        y_sb = _attn_prompt(p2, p["u"][tq], b, t, tq, ATTN_HEADS_PER_STEP)
    else:
        tc = 256
        y_sb = _attn_sample(p2, k_past, v_past, p["u"][t], p["u"][tc], b, t, k_past.shape[1], tc,
                            ATTN_HEADS_PER_STEP)

    merged = _merge(y_ssd.reshape(m, D_INNER), y_sb.reshape(m, D_ATTN), p["w_br_ssd"], p["w_br_attn"], p1,
                    min(m, 1024), 512)
    x1 = _outproj_ln(merged, p["w_out"], x2, p["ln1_g"], p["ln1_b"], min(m, 512))
    x_out = _mlp_ln(x1, p["w_up"], p["b_up"], p["w_down"], p["b_down"], p["ln2_g"], p["ln2_b"], min(m, 512), 1024)
    kv_shape = (b, t, SB_HEADS, SB_HEADDIM)
    return x_out.reshape(b, t, D_MODEL), conv_new, h_new, k.reshape(kv_shape), v.reshape(kv_shape)


def _suffix_sum_matrix(n):
    u = jnp.tril(jnp.ones((n, n), F32)).astype(BF16)
    return jnp.concatenate([u, u], axis=0)


def _layer_params(l, w_in, b_gate, conv_w, conv_b, dt_bias, a_log, d_skip, ssm_norm_w, w_br_ssd, w_br_attn, w_out,
                  ln1_g, ln1_b, w_up, b_up, w_down, b_down, ln2_g, ln2_b):
    w = w_in[l]
    o_dt = D_INNER + CONV_DIM
    o_q = o_dt + SSD_HEADS
    w_dt = jnp.pad(w[:, o_dt:o_q], ((0, 0), (0, 128 - SSD_HEADS))).astype(BF16)
    row = lambda a: a[l].reshape(1, -1)
    head_of_lane = jnp.arange(D_INNER) // SSD_HEADDIM
    return {
        "w_a": w[:, :o_dt].astype(BF16),
        "w_b": w[:, o_q:].astype(BF16),
        "w_dt": w_dt,
        "b_gate": row(b_gate),
        "conv_w": conv_w[l],
        "conv_b": row(conv_b),
        "dt_bias": row(dt_bias),
        "a_exp": jnp.repeat(-jnp.exp(a_log[l].astype(F32)), SSD_HEADDIM).reshape(1, D_INNER),
        "dskip_exp": jnp.repeat(d_skip[l], SSD_HEADDIM).reshape(1, D_INNER),
        "norm_w": row(ssm_norm_w),
        "e_mat": jnp.tile(jnp.arange(SSD_HEADS)[:, None] == head_of_lane[None, :], (2, 1)).astype(BF16),
        "tri": jnp.tile(jnp.tril(jnp.ones((CHUNK, CHUNK), F32)), (1, 3)).astype(BF16),
        "u": {n: _suffix_sum_matrix(n) for n in (64, 256)},
        "w_br_ssd": w_br_ssd[l].astype(BF16),
        "w_br_attn": w_br_attn[l].astype(BF16),
        "w_out": w_out[l].astype(BF16),
        "ln1_g": row(ln1_g),
        "ln1_b": row(ln1_b),
        "w_up": w_up[l].astype(BF16),
        "b_up": row(b_up),
        "w_down": w_down[l].astype(BF16),
        "b_down": row(b_down),
        "ln2_g": row(ln2_g),
        "ln2_b": row(ln2_b),
    }


def kernel(x_prompt, x_sample, cache_conv, state_ssm, cache_k, cache_v, w_in, b_gate, conv_w, conv_b, dt_bias, a_log, d_skip, ssm_norm_w, w_br_ssd, w_br_attn, w_out, ln1_g, ln1_b, w_up, b_up, w_down, b_down, ln2_g, ln2_b):
    hp, hs = x_prompt, x_sample
    outs_p, outs_s = [], []
    for l in range(DEPTH):
        p = _layer_params(l, w_in, b_gate, conv_w, conv_b, dt_bias, a_log, d_skip, ssm_norm_w, w_br_ssd, w_br_attn,
                          w_out, ln1_g, ln1_b, w_up, b_up, w_down, b_down, ln2_g, ln2_b)
        hp, c1, s1, k1, v1 = _trunk(hp, None, None, None, None, p)
        outs_p.append((c1, s1, k1, v1))
        hs, c2, s2, k2, v2 = _trunk(hs, cache_conv[l], state_ssm[l], cache_k[l], cache_v[l], p)
        outs_s.append((c2, s2, k2, v2))
    stack = lambda outs, i: jnp.stack([o[i] for o in outs])
    return (hp, hs, stack(outs_p, 0), stack(outs_p, 1), stack(outs_p, 2), stack(outs_p, 3),
            stack(outs_s, 0), stack(outs_s, 1), stack(outs_s, 2), stack(outs_s, 3))
```

```python
import functools
import math

import jax
import jax.numpy as jnp
from jax import lax
from jax.experimental import pallas as pl
from jax.experimental.pallas import tpu as pltpu

D_MODEL = 2048
D_INNER = 2 * D_MODEL
SSD_HEADDIM = 64
SSD_HEADS = D_INNER // SSD_HEADDIM
SSD_GROUPS = 8
HEADS_PER_GROUP = SSD_HEADS // SSD_GROUPS
GROUP_W = D_INNER // SSD_GROUPS
D_STATE = 128
CONV_W = 4
CONV_DIM = D_INNER + 2 * SSD_GROUPS * D_STATE
SB_HEADDIM = 128
SB_HEADS = D_MODEL // SB_HEADDIM
D_ATTN = SB_HEADS * SB_HEADDIM
SB_SCALE = 1.0 / math.sqrt(SB_HEADDIM)
D_FF = 4 * D_MODEL
DEPTH = 1
ALPHA = (2.0 * DEPTH) ** 0.25
LOG2E = math.log2(math.e)
SP2_CLAMP = 126.0
R_STOP = 160.0
LN_EPS = 1e-5
RMS_EPS = 1e-5
CHUNK = 64
ATTN_HEADS_PER_STEP = 4
ATTN_Q_TILES_PER_STEP = 2
SSD_GROUP_UNROLL = 4
SSD_CHUNKS_PER_STEP = 4
CONV_ROWS = 128

V7X_VMEM_BYTES = 64 * 1024 * 1024
VMEM_LIMIT = 56 * 1024 * 1024

F32 = jnp.float32
BF16 = jnp.bfloat16


def _cparams(sem):
    return pltpu.CompilerParams(dimension_semantics=sem, vmem_limit_bytes=VMEM_LIMIT)


def _sigmoid(x):
    return 1.0 / (1.0 + jnp.exp(-x))


def _softplus(x):
    return jnp.maximum(x, 0.0) + jnp.log1p(jnp.exp(-jnp.abs(x)))


def _split_bf16(x):
    hi = x.astype(BF16)
    lo = (x - hi.astype(F32)).astype(BF16)
    return hi, lo


def _split3_bf16(x):
    hi = x.astype(BF16)
    r1 = x - hi.astype(F32)
    mid = r1.astype(BF16)
    lo = (r1 - mid.astype(F32)).astype(BF16)
    return hi, mid, lo


def _layer_norm(x, g, b):
    mu = jnp.mean(x, axis=-1, keepdims=True)
    xc = x - mu
    var = jnp.mean(xc * xc, axis=-1, keepdims=True)
    return xc * lax.rsqrt(var + LN_EPS) * g + b


P1_XBC, P1_ZS, P1_GATE, P1_W = 0, 8192, 12288, 16384
P2_Q, P2_K, P2_V, P2_W = 0, D_ATTN, 2 * D_ATTN, 3 * D_ATTN


def _inproj_kernel(*refs, ends, seq_tiles):
    if seq_tiles is None:
        x_ref, wa_ref, wb_ref, wdt_ref, bg_ref, p1_ref, k_ref, v_ref, p2_ref, dt_ref = refs
    else:
        (x_ref, wa_ref, wb_ref, wdt_ref, bg_ref, cw_ref, cb_ref,
         p1_ref, k_ref, v_ref, p2_ref, dt_ref, tail_ref, hist) = refs
    i, j = pl.program_id(0), pl.program_id(1)
    ez, exbc, eq, ek, ev, _ = ends
    tm = x_ref.shape[0]

    def proj(w_ref):
        return jnp.dot(x_ref[...], w_ref[...], preferred_element_type=F32)

    @pl.when(j == 0)
    def _():
        dt_ref[...] = jnp.dot(x_ref[...], wdt_ref[...], preferred_element_type=F32)

    @pl.when(j < ez)
    def _():
        acc = proj(wa_ref)
        p1_ref[...] = acc * _sigmoid(acc)

    @pl.when((j >= ez) & (j < exbc))
    def _():
        if seq_tiles is None:
            p1_ref[...] = proj(wa_ref)
        else:
            jx = j - ez
            prev = jnp.where(i % seq_tiles == 0, 0.0, hist[jx])
            w = wa_ref[...]
            for r0 in range(0, tm, CONV_ROWS):
                raw = jnp.dot(x_ref[r0:r0 + CONV_ROWS, :], w, preferred_element_type=F32)
                ext = jnp.concatenate([prev, raw], axis=0)
                acc = cb_ref[...] + cw_ref[CONV_W - 1:CONV_W, :] * raw
                for t in range(CONV_W - 1):
                    acc = acc + cw_ref[t:t + 1, :] * ext[5 + t:5 + t + CONV_ROWS, :]
                p1_ref[r0:r0 + CONV_ROWS, :] = acc * _sigmoid(acc)
                prev = raw[CONV_ROWS - 8:, :]
            hist[jx] = prev
            tail_ref[0] = prev

    @pl.when((j >= exbc) & (j < eq))
    def _():
        p2_ref[...] = (proj(wb_ref) * (SB_SCALE * LOG2E)).astype(BF16)

    @pl.when((j >= eq) & (j < ek))
    def _():
        acc = proj(wb_ref)
        k_ref[...] = acc
        p2_ref[...] = acc.astype(BF16)

    @pl.when((j >= ek) & (j < ev))
    def _():
        acc = proj(wb_ref)
        v_ref[...] = acc
        p2_ref[...] = acc.astype(BF16)

    @pl.when(j >= ev)
    def _():
        p1_ref[...] = _sigmoid(proj(wb_ref) + bg_ref[...])


def _in_proj(xb, w_a, w_b, w_dt, b_gate, conv_w, conv_b, tm, tn, seq_tiles):
    m = xb.shape[0]
    widths = (D_INNER, CONV_DIM, D_ATTN, D_ATTN, D_ATTN, 2 * D_MODEL)
    ends, e = [], 0
    for w in widths:
        assert w % tn == 0
        e += w // tn
        ends.append(e)
    ez, exbc, eq, ek, ev, eg = ends
    nxbc = CONV_DIM // tn

    def p1_map(i, j):
        blk = jnp.where(j < ez, j + P1_ZS // tn,
                        jnp.where(j < exbc, j - ez, jnp.where(j < ev, nxbc - 1, j - ev + P1_GATE // tn)))
        return (i, blk)

    xbc_map = lambda i, j: (0, jnp.clip(j - ez, 0, nxbc - 1))
    in_specs = [
        pl.BlockSpec((tm, D_MODEL), lambda i, j: (i, 0), pipeline_mode=pl.Buffered(1)),
        pl.BlockSpec((D_MODEL, tn), lambda i, j: (0, jnp.minimum(j, exbc - 1))),
        pl.BlockSpec((D_MODEL, tn), lambda i, j: (0, jnp.maximum(j - exbc, 0))),
        pl.BlockSpec((D_MODEL, 128), lambda i, j: (0, 0)),
        pl.BlockSpec((1, tn), lambda i, j: (0, jnp.clip(j - ev, 0, eg - ev - 1))),
    ]
    args = [xb, w_a, w_b, w_dt, b_gate]
    out_shapes = [
        jax.ShapeDtypeStruct((m, P1_W), F32),
        jax.ShapeDtypeStruct((m, D_ATTN), F32),
        jax.ShapeDtypeStruct((m, D_ATTN), F32),
        jax.ShapeDtypeStruct((m, P2_W), BF16),
        jax.ShapeDtypeStruct((m, 128), F32),
    ]
    out_specs = [
        pl.BlockSpec((tm, tn), p1_map),
        pl.BlockSpec((tm, tn), lambda i, j: (i, jnp.clip(j - eq, 0, ek - eq - 1))),
        pl.BlockSpec((tm, tn), lambda i, j: (i, jnp.clip(j - ek, 0, ev - ek - 1))),
        pl.BlockSpec((tm, tn), lambda i, j: (i, jnp.clip(j - exbc, 0, ev - exbc - 1))),
        pl.BlockSpec((tm, 128), lambda i, j: (i, 0)),
    ]
    scratch = []
    if seq_tiles is not None:
        in_specs += [pl.BlockSpec((CONV_W, tn), xbc_map), pl.BlockSpec((1, tn), xbc_map)]
        args += [conv_w, conv_b]
        out_shapes.append(jax.ShapeDtypeStruct((m // tm, 8, CONV_DIM), F32))
        out_specs.append(pl.BlockSpec((1, 8, tn), lambda i, j: (i, 0, jnp.clip(j - ez, 0, nxbc - 1))))
        scratch.append(pltpu.VMEM((nxbc, 8, tn), F32))
    return pl.pallas_call(
        functools.partial(_inproj_kernel, ends=tuple(ends), seq_tiles=seq_tiles),
        grid=(m // tm, eg),
        in_specs=in_specs,
        out_specs=out_specs,
        out_shape=out_shapes,
        scratch_shapes=scratch,
        compiler_params=_cparams(("arbitrary", "arbitrary")),
        name="in_proj",
    )(*args)


def _ssd_kernel(*refs, has_history, n_sub):
    if has_history:
        (xbc_ref, dtr_ref, z_ref, cprev_ref, h0_ref, convw_ref, convb_ref, dtb_ref, a2_ref, dskip_ref,
         nw_ref, e2_ref, e3_ref, tri_ref, y_ref, ht_ref, xpad, state) = refs
    else:
        (xbc_ref, dtr_ref, z_ref, dtb_ref, a2_ref, dskip_ref,
         nw_ref, e2_ref, e3_ref, tri_ref, y_ref, ht_ref, state) = refs
    c = pl.program_id(1)
    nc = pl.num_programs(1)
    L = CHUNK

    @pl.when(c == 0)
    def _():
        if has_history:
            xpad[0:8, :] = cprev_ref[0]
            state[...] = h0_ref[0]
        else:
            state[...] = jnp.zeros(state.shape, F32)

    tri3 = tri_ref[...]
    row = lax.broadcasted_iota(jnp.int32, (L, GROUP_W), 0)
    lane = lax.broadcasted_iota(jnp.int32, (L, GROUP_W), 1)
    pos_in_head = lane & (SSD_HEADDIM - 1)
    diag_sel = row == pos_in_head
    causal = row >= pos_in_head
    lane128 = lax.broadcasted_iota(jnp.int32, (L, 128), 1)
    lo_half = lane128 < SSD_HEADDIM
    zero = jnp.zeros((L, 128), BF16)

    def chunk_body(ci, carry):
        rows = pl.ds(pl.multiple_of(ci * L, L), L)
        if has_history:
            xpad[8:8 + L, :] = xbc_ref[0, rows, :]

        def conv_cols(start, width):
            sl = pl.ds(start, width)
            if not has_history:
                return xbc_ref[0, rows, sl]
            acc = convb_ref[:, sl] + convw_ref[CONV_W - 1:CONV_W, sl] * xpad[8:8 + L, sl]
            for t in range(CONV_W - 1):
                acc = acc + convw_ref[t:t + 1, sl] * xpad[5 + t:5 + t + L, sl]
            return acc * _sigmoid(acc)

        dt = _softplus(dtr_ref[0, rows, :][:, :SSD_HEADS] + dtb_ref[...])
        acs_c = jnp.dot(tri3, jnp.concatenate(_split3_bf16(dt * a2_ref[...]), axis=0),
                        preferred_element_type=F32)
        decay_in = jnp.exp2(acs_c)
        decay_out = jnp.exp2(acs_c[L - 1:L, :] - acs_c)
        lhs2 = jnp.concatenate([jnp.concatenate(_split_bf16(v), axis=1) for v in (dt, decay_in, decay_out)],
                               axis=0)
        lhs3 = jnp.concatenate(_split3_bf16(acs_c), axis=1)
        groups_body = functools.partial(_ssd_groups, refs=(z_ref, dskip_ref, nw_ref, e2_ref, e3_ref, y_ref, state),
                                        conv_cols=conv_cols, rows=rows, lhs2=lhs2, lhs3=lhs3,
                                        masks=(diag_sel, causal, lo_half, zero))
        lax.fori_loop(0, SSD_GROUPS // SSD_GROUP_UNROLL, groups_body, 0)
        if has_history:
            xpad[0:8, :] = xpad[L:L + 8, :]
        return carry

    lax.fori_loop(0, n_sub, chunk_body, 0)

    @pl.when(c == nc - 1)
    def _():
        ht_ref[0] = state[...]


def _ssd_groups(it, carry, *, refs, conv_cols, rows, lhs2, lhs3, masks):
    z_ref, dskip_ref, nw_ref, e2_ref, e3_ref, y_ref, state = refs
    diag_sel, causal, lo_half, zero = masks
    L = CHUNK
    gs = [it * SSD_GROUP_UNROLL + k for k in range(SSD_GROUP_UNROLL)]
    xoffs = [pl.multiple_of(g * GROUP_W, GROUP_W) for g in gs]
    xss, bgs, cgs = [], [], []
    for g, xoff in zip(gs, xoffs):
        boff = pl.multiple_of(D_INNER + g * D_STATE, D_STATE)
        coff = pl.multiple_of(D_INNER + SSD_GROUPS * D_STATE + g * D_STATE, D_STATE)
        xss.append(conv_cols(xoff, GROUP_W))
        bgs.append(conv_cols(boff, D_STATE).astype(BF16))
        cgs.append(conv_cols(coff, D_STATE).astype(BF16))

    spreads = [jnp.dot(lhs2, e2_ref[:, pl.ds(xoff, GROUP_W)], preferred_element_type=F32) for xoff in xoffs]
    acss = [jnp.dot(lhs3, e3_ref[:, pl.ds(xoff, GROUP_W)], preferred_element_type=F32) for xoff in xoffs]
    cb2s = [lax.dot_general(cg, jnp.concatenate([bg, bg], axis=0), (((1,), (1,)), ((), ())),
                            preferred_element_type=F32) for cg, bg in zip(cgs, bgs)]

    decay_ins, xdds, mmats, bds = [], [], [], []
    for xs, spread, acs, cb2 in zip(xss, spreads, acss, cb2s):
        decay_ins.append(spread[L:2 * L])
        xd = xs * spread[0:L]
        xdds.append((xd * spread[2 * L:3 * L]).astype(BF16))
        xdb = xd.astype(BF16)
        rowform = jnp.sum(jnp.where(diag_sel, acs, 0.0), axis=0, keepdims=True)
        lmat = jnp.where(causal, jnp.exp2(acs - rowform), 0.0)
        mmats.append((lmat * jnp.concatenate([cb2] * (GROUP_W // 128), axis=1)).astype(BF16))
        bds.append([jnp.concatenate([jnp.where(lo_half, xdb[:, 128 * p:128 * (p + 1)], zero),
                                     jnp.where(lo_half, zero, xdb[:, 128 * p:128 * (p + 1)])], axis=0)
                    for p in range(GROUP_W // 128)])

    sts = [state[g] for g in gs]
    ydiags = [jnp.concatenate([jnp.dot(mmat[:, 128 * p:128 * (p + 1)], bd[p], preferred_element_type=F32)
                               for p in range(GROUP_W // 128)], axis=1) for mmat, bd in zip(mmats, bds)]
    yoffs = [jnp.dot(cg, st.astype(BF16), preferred_element_type=F32) for cg, st in zip(cgs, sts)]
    snews = [lax.dot_general(bg, xdd, (((0,), (0,)), ((), ())), preferred_element_type=F32)
             for bg, xdd in zip(bgs, xdds)]

    for k, (g, xoff) in enumerate(zip(gs, xoffs)):
        cols = pl.ds(xoff, GROUP_W)
        y = ydiags[k] + yoffs[k] * decay_ins[k] + dskip_ref[:, cols] * xss[k]
        state[g] = sts[k] * decay_ins[k][L - 1:L, :] + snews[k]
        yg = y * z_ref[0, rows, cols]
        ms = jnp.mean(yg * yg, axis=-1, keepdims=True)
        y_ref[0, rows, cols] = (yg * lax.rsqrt(ms + RMS_EPS) * nw_ref[:, cols]).astype(y_ref.dtype)
    return carry


def _ssd(p1, dtr, conv_prev8, h0t, conv_w, conv_b, dt_bias, a2, dskip_exp, norm_w, e2, e3, tri, b, t):
    has_history = conv_prev8 is not None
    n_sub = min(SSD_CHUNKS_PER_STEP, t // CHUNK)
    rows = n_sub * CHUNK
    assert t % rows == 0
    tok = lambda bi, c: (bi, c, 0)
    const2 = lambda bi, c: (0, 0)
    p1 = p1.reshape(b, t, P1_W)
    in_specs = [
        pl.BlockSpec((1, rows, CONV_DIM), lambda bi, c: (bi, c, P1_XBC // CONV_DIM)),
        pl.BlockSpec((1, rows, 128), tok),
        pl.BlockSpec((1, rows, D_INNER), lambda bi, c: (bi, c, P1_ZS // D_INNER)),
    ]
    args = [p1, dtr.reshape(b, t, 128), p1]
    if has_history:
        in_specs += [
            pl.BlockSpec((1, 8, CONV_DIM), lambda bi, c: (bi, 0, 0)),
            pl.BlockSpec((1, SSD_GROUPS, D_STATE, GROUP_W), lambda bi, c: (bi, 0, 0, 0)),
            pl.BlockSpec((CONV_W, CONV_DIM), const2),
            pl.BlockSpec((1, CONV_DIM), const2),
        ]
        args += [conv_prev8, h0t, conv_w, conv_b]
    in_specs += [
        pl.BlockSpec((1, SSD_HEADS), const2),
        pl.BlockSpec((1, SSD_HEADS), const2),
        pl.BlockSpec((1, D_INNER), const2),
        pl.BlockSpec((1, D_INNER), const2),
        pl.BlockSpec((2 * SSD_HEADS, D_INNER), const2),
        pl.BlockSpec((3 * SSD_HEADS, D_INNER), const2),
        pl.BlockSpec((CHUNK, 3 * CHUNK), const2),
    ]
    args += [dt_bias, a2, dskip_exp, norm_w, e2, e3, tri]
    scratch = [pltpu.VMEM((SSD_GROUPS, D_STATE, GROUP_W), F32)]
    if has_history:
        scratch.insert(0, pltpu.VMEM((8 + CHUNK, CONV_DIM), F32))
    return pl.pallas_call(
        functools.partial(_ssd_kernel, has_history=has_history, n_sub=n_sub),
        grid=(b, t // rows),
        in_specs=in_specs,
        out_specs=[
            pl.BlockSpec((1, rows, D_INNER), tok),
            pl.BlockSpec((1, SSD_GROUPS, D_STATE, GROUP_W), lambda bi, c: (bi, 0, 0, 0)),
        ],
        out_shape=(
            jax.ShapeDtypeStruct((b, t, D_INNER), BF16),
            jax.ShapeDtypeStruct((b, SSD_GROUPS, D_STATE, GROUP_W), F32),
        ),
        scratch_shapes=scratch,
        compiler_params=_cparams(("parallel", "arbitrary")),
        name="conv_ssd",
    )(*args)


def _sb_tiles(qs, kbs, vbs, u, carry, masked):
    n = len(qs)
    ts = [lax.dot_general(qs[h], kbs[h], (((1,), (1,)), ((), ())), preferred_element_type=F32) for h in range(n)]
    if masked:
        tq, tk = ts[0].shape
        visible = lax.broadcasted_iota(jnp.int32, (tq, tk), 1) < lax.broadcasted_iota(jnp.int32, (tq, tk), 0)
    sp2s, hilos = [], []
    for t in ts:
        sp2 = jnp.maximum(t, jnp.log(1.0 + jnp.exp2(jnp.minimum(t, SP2_CLAMP))) * LOG2E)
        if masked:
            sp2 = jnp.where(visible, sp2, 0.0)
        sp2s.append(sp2)
        hilos.append(jnp.concatenate(_split_bf16(sp2), axis=1))
    sums = [jnp.dot(hl, u, preferred_element_type=F32) for hl in hilos]
    ws = []
    for t, s in zip(ts, sums):
        w = jnp.exp2(t - s)
        if masked:
            w = jnp.where(visible, w, 0.0)
        ws.append(w.astype(BF16))
    pvs = [jnp.dot(ws[h], vbs[h], preferred_element_type=F32) for h in range(n)]
    out = []
    for h in range(n):
        r, acc = carry[h]
        out.append((r + jnp.sum(sp2s[h], axis=-1, keepdims=True), acc + jnp.exp2(-r) * pvs[h]))
    return tuple(out)


def _visit_key_blocks(step, n_blocks, carry):
    def r_min(carry):
        return functools.reduce(jnp.minimum, [jnp.min(r) for r, _ in carry])

    def cond(state):
        i, rmin, _ = state
        return jnp.logical_and(i < n_blocks, rmin < R_STOP)

    def body(state):
        i, _, carry = state
        carry = step(i, carry)
        return i + 1, r_min(carry), carry

    return lax.while_loop(cond, body, (jnp.int32(0), r_min(carry), carry))[2]


def _head_cols(hh):
    return slice(hh * SB_HEADDIM, (hh + 1) * SB_HEADDIM)


def _attn_prompt_kernel(q_ref, k_ref, v_ref, u_ref, o_ref, *, tq, hp, n_q):
    u = u_ref[...]
    for qq in range(n_q):
        qi = pl.program_id(2) * n_q + qq
        rows = slice(qq * tq, (qq + 1) * tq)
        qs = [q_ref[0, rows, _head_cols(hh)] for hh in range(hp)]

        def tiles(jb, carry, masked, qs=qs):
            off = pl.multiple_of(jb * tq, tq)
            kbs = [k_ref[0, pl.ds(off, tq), _head_cols(hh)] for hh in range(hp)]
            vbs = [v_ref[0, pl.ds(off, tq), _head_cols(hh)] for hh in range(hp)]
            return _sb_tiles(qs, kbs, vbs, u, carry, masked)

        init = tuple((jnp.zeros((tq, 1), F32), jnp.zeros((tq, SB_HEADDIM), F32)) for _ in range(hp))
        carry = _visit_key_blocks(lambda i, c, qi=qi, tiles=tiles: tiles(qi - 1 - i, c, False), qi,
                                  tiles(qi, init, True))
        for hh in range(hp):
            o_ref[0, rows, _head_cols(hh)] = carry[hh][1].astype(o_ref.dtype)


def _attn_prompt(p2, u, b, t, tq, hp):
    q3 = k3 = v3 = p2.reshape(b, t, P2_W)
    w = hp * SB_HEADDIM
    n_q = ATTN_Q_TILES_PER_STEP
    return pl.pallas_call(
        functools.partial(_attn_prompt_kernel, tq=tq, hp=hp, n_q=n_q),
        grid=(b, SB_HEADS // hp, t // (tq * n_q)),
        in_specs=[
            pl.BlockSpec((1, tq * n_q, w), lambda bi, h, qi: (bi, qi, P2_Q // w + h)),
            pl.BlockSpec((1, t, w), lambda bi, h, qi: (bi, 0, P2_K // w + h)),
            pl.BlockSpec((1, t, w), lambda bi, h, qi: (bi, 0, P2_V // w + h)),
            pl.BlockSpec((2 * tq, tq), lambda bi, h, qi: (0, 0)),
        ],
        out_specs=pl.BlockSpec((1, tq * n_q, w), lambda bi, h, qi: (bi, qi, h)),
        out_shape=jax.ShapeDtypeStruct((b, t, D_ATTN), BF16),
        compiler_params=_cparams(("parallel", "parallel", "arbitrary")),
        name="sb_attn_prompt",
    )(q3, k3, v3, u)


def _attn_sample_kernel(q_ref, kn_ref, vn_ref, kc_ref, vc_ref, un_ref, uc_ref, o_ref, *, tn, tc, n_cache, hp, hb):
    un, uc = un_ref[...], uc_ref[...]
    for h0 in range(0, hb, hp):
        heads = range(h0, h0 + hp)
        qs = [q_ref[0, :, _head_cols(hh)] for hh in heads]
        init = tuple((jnp.zeros((tn, 1), F32), jnp.zeros((tn, SB_HEADDIM), F32)) for _ in heads)
        carry = _sb_tiles(qs, [kn_ref[0, :, _head_cols(hh)] for hh in heads],
                          [vn_ref[0, :, _head_cols(hh)] for hh in heads], un, init, True)

        def body(i, carry, heads=heads, qs=qs):
            off = pl.multiple_of((n_cache - 1 - i) * tc, tc)
            kbs = [kc_ref[0, pl.ds(off, tc), hh, :].astype(BF16) for hh in heads]
            vbs = [vc_ref[0, pl.ds(off, tc), hh, :].astype(BF16) for hh in heads]
            return _sb_tiles(qs, kbs, vbs, uc, carry, False)

        carry = _visit_key_blocks(body, n_cache, carry)
        for hh, (_, acc) in zip(heads, carry):
            o_ref[0, :, _head_cols(hh)] = acc.astype(o_ref.dtype)


def _attn_sample(p2, cache_k, cache_v, un, uc, b, t, past, tc, hp):
    q3 = k3 = v3 = p2.reshape(b, t, P2_W)
    hb = 8
    w = hb * SB_HEADDIM
    new = lambda bi, h: (bi, 0, h)
    cache = lambda bi, h: (bi, 0, h, 0)
    return pl.pallas_call(
        functools.partial(_attn_sample_kernel, tn=t, tc=tc, n_cache=past // tc, hp=hp, hb=hb),
        grid=(b, SB_HEADS // hb),
        in_specs=[
            pl.BlockSpec((1, t, w), lambda bi, h: (bi, 0, P2_Q // w + h)),
            pl.BlockSpec((1, t, w), lambda bi, h: (bi, 0, P2_K // w + h)),
            pl.BlockSpec((1, t, w), lambda bi, h: (bi, 0, P2_V // w + h)),
            pl.BlockSpec((1, past, hb, SB_HEADDIM), cache),
            pl.BlockSpec((1, past, hb, SB_HEADDIM), cache),
            pl.BlockSpec((2 * t, t), lambda bi, h: (0, 0)),
            pl.BlockSpec((2 * tc, tc), lambda bi, h: (0, 0)),
        ],
        out_specs=pl.BlockSpec((1, t, w), new),
        out_shape=jax.ShapeDtypeStruct((b, t, D_ATTN), BF16),
        compiler_params=_cparams(("parallel", "parallel")),
        name="sb_attn_sample",
    )(q3, k3, v3, cache_k, cache_v, un, uc)


def _merge_kernel(ya_ref, yb_ref, wa_ref, wb_ref, ga_ref, gb_ref, o_ref):
    a = jnp.dot(ya_ref[...], wa_ref[...], preferred_element_type=F32)
    bb = jnp.dot(yb_ref[...], wb_ref[...], preferred_element_type=F32)
    o_ref[...] = (ga_ref[...] * a + gb_ref[...] * bb).astype(o_ref.dtype)


def _merge(ya, yb, wa, wb, p1, tm, tn):
    m = ya.shape[0]
    nb = D_MODEL // tn
    g0 = P1_GATE // tn
    return pl.pallas_call(
        _merge_kernel,
        grid=(m // tm, nb),
        in_specs=[
            pl.BlockSpec((tm, D_INNER), lambda i, j: (i, 0)),
            pl.BlockSpec((tm, D_ATTN), lambda i, j: (i, 0)),
            pl.BlockSpec((D_INNER, tn), lambda i, j: (0, j)),
            pl.BlockSpec((D_ATTN, tn), lambda i, j: (0, j)),
            pl.BlockSpec((tm, tn), lambda i, j: (i, g0 + j)),
            pl.BlockSpec((tm, tn), lambda i, j: (i, g0 + nb + j)),
        ],
        out_specs=pl.BlockSpec((tm, tn), lambda i, j: (i, j)),
        out_shape=jax.ShapeDtypeStruct((m, D_MODEL), BF16),
        compiler_params=_cparams(("parallel", "arbitrary")),
        name="branch_merge",
    )(ya, yb, wa, wb, p1, p1)


def _outproj_kernel(m_ref, w_ref, x_ref, g_ref, b_ref, o_ref):
    mix = jnp.dot(m_ref[...], w_ref[...], preferred_element_type=F32)
    o_ref[...] = _layer_norm(ALPHA * x_ref[...] + mix, g_ref[...], b_ref[...])


def _outproj_ln(merged, w_out, x, g, bb, tm):
    m = x.shape[0]
    row = lambda i: (i, 0)
    const = lambda i: (0, 0)
    return pl.pallas_call(
        _outproj_kernel,
        grid=(m // tm,),
        in_specs=[
            pl.BlockSpec((tm, D_MODEL), row),
            pl.BlockSpec((D_MODEL, D_MODEL), const),
            pl.BlockSpec((tm, D_MODEL), row),
            pl.BlockSpec((1, D_MODEL), const),
            pl.BlockSpec((1, D_MODEL), const),
        ],
        out_specs=pl.BlockSpec((tm, D_MODEL), row),
        out_shape=jax.ShapeDtypeStruct((m, D_MODEL), F32),
        compiler_params=_cparams(("parallel",)),
        name="outproj_ln",
    )(merged, w_out, x, g, bb)


def _mlp_kernel(x_ref, wu_ref, bu_ref, wd_ref, bd_ref, g_ref, b_ref, o_ref, xb, acc):
    f = pl.program_id(1)

    @pl.when(f == 0)
    def _():
        xb[...] = x_ref[...].astype(BF16)
        acc[...] = jnp.zeros(acc.shape, F32)

    h = jnp.dot(xb[...], wu_ref[...], preferred_element_type=F32) + bu_ref[...]
    h = jnp.square(jnp.maximum(h, 0.0)).astype(BF16)
    acc[...] += jnp.dot(h, wd_ref[...], preferred_element_type=F32)

    @pl.when(f == pl.num_programs(1) - 1)
    def _():
        o_ref[...] = _layer_norm(ALPHA * x_ref[...] + acc[...] + bd_ref[...], g_ref[...], b_ref[...])


def _mlp_ln(x1, w_up, b_up, w_down, b_down, g, bb, tm, tf):
    m = x1.shape[0]
    row = lambda i, f: (i, 0)
    const = lambda i, f: (0, 0)
    return pl.pallas_call(
        _mlp_kernel,
        grid=(m // tm, D_FF // tf),
        in_specs=[
            pl.BlockSpec((tm, D_MODEL), row),
            pl.BlockSpec((D_MODEL, tf), lambda i, f: (0, f)),
            pl.BlockSpec((1, tf), lambda i, f: (0, f)),
            pl.BlockSpec((tf, D_MODEL), lambda i, f: (f, 0)),
            pl.BlockSpec((1, D_MODEL), const),
            pl.BlockSpec((1, D_MODEL), const),
            pl.BlockSpec((1, D_MODEL), const),
        ],
        out_specs=pl.BlockSpec((tm, D_MODEL), row),
        out_shape=jax.ShapeDtypeStruct((m, D_MODEL), F32),
        scratch_shapes=[pltpu.VMEM((tm, D_MODEL), BF16), pltpu.VMEM((tm, D_MODEL), F32)],
        compiler_params=_cparams(("parallel", "arbitrary")),
        name="mlp_ln",
    )(x1, w_up, b_up, w_down, b_down, g, bb)


def _trunk(x, conv_prev, h0, k_past, v_past, p):
    b, t, _ = x.shape
    m = b * t
    x2 = x.reshape(m, D_MODEL)
    tm = min(m, 1024)
    keep = CONV_W - 1
    if conv_prev is None:
        assert t % tm == 0
        p1, k, v, p2, dtr, tails = _in_proj(x2.astype(BF16), p["w_a"], p["w_b"], p["w_dt"], p["b_gate"],
                                            p["conv_w"], p["conv_b"], tm, 1024, t // tm)
        conv_prev8, h0t = None, None
        conv_new = tails.reshape(b, t // tm, 8, CONV_DIM)[:, -1, 8 - keep:, :]
    else:
        p1, k, v, p2, dtr = _in_proj(x2.astype(BF16), p["w_a"], p["w_b"], p["w_dt"], p["b_gate"],
                                     None, None, tm, 1024, None)
        conv_prev8 = jnp.pad(conv_prev, ((0, 0), (8 - keep, 0), (0, 0)))
        h0t = h0.reshape(b, SSD_GROUPS, GROUP_W, D_STATE).transpose(0, 1, 3, 2)
        conv_new = p1.reshape(b, t, P1_W)[:, t - keep:, P1_XBC:P1_XBC + CONV_DIM]
    y_ssd, ht = _ssd(p1, dtr, conv_prev8, h0t, p["conv_w"], p["conv_b"], p["dt_bias"], p["a2"],
                     p["dskip_exp"], p["norm_w"], p["e2"], p["e3"], p["tri"], b, t)
    h_new = ht.transpose(0, 1, 3, 2).reshape(b, SSD_HEADS, SSD_HEADDIM, D_STATE)

    if k_past is None:
        tq = 256
        y_sb = _attn_prompt(p2, p["u"][tq], b, t, tq, ATTN_HEADS_PER_STEP)
    else:
        tc = 256
        y_sb = _attn_sample(p2, k_past, v_past, p["u"][t], p["u"][tc], b, t, k_past.shape[1], tc,
                            ATTN_HEADS_PER_STEP)

    merged = _merge(y_ssd.reshape(m, D_INNER), y_sb.reshape(m, D_ATTN), p["w_br_ssd"], p["w_br_attn"], p1,
                    min(m, 1024), 512)
    x1 = _outproj_ln(merged, p["w_out"], x2, p["ln1_g"], p["ln1_b"], min(m, 512))
    x_out = _mlp_ln(x1, p["w_up"], p["b_up"], p["w_down"], p["b_down"], p["ln2_g"], p["ln2_b"], min(m, 512), 1024)
    kv_shape = (b, t, SB_HEADS, SB_HEADDIM)
    return x_out.reshape(b, t, D_MODEL), conv_new, h_new, k.reshape(kv_shape), v.reshape(kv_shape)


def _suffix_sum_matrix(n):
    u = jnp.tril(jnp.ones((n, n), F32)).astype(BF16)
    return jnp.concatenate([u, u], axis=0)


def _layer_params(l, w_in, b_gate, conv_w, conv_b, dt_bias, a_log, d_skip, ssm_norm_w, w_br_ssd, w_br_attn, w_out,
                  ln1_g, ln1_b, w_up, b_up, w_down, b_down, ln2_g, ln2_b):
    w = w_in[l]
    o_dt = D_INNER + CONV_DIM
    o_q = o_dt + SSD_HEADS
    w_dt = jnp.pad(w[:, o_dt:o_q], ((0, 0), (0, 128 - SSD_HEADS))).astype(BF16)
    row = lambda a: a[l].reshape(1, -1)
    head_to_lanes = jnp.arange(SSD_HEADS)[:, None] == (jnp.arange(D_INNER) // SSD_HEADDIM)[None, :]
    return {
        "w_a": w[:, :o_dt].astype(BF16),
        "w_b": w[:, o_q:].astype(BF16),
        "w_dt": w_dt,
        "b_gate": row(b_gate),
        "conv_w": conv_w[l],
        "conv_b": row(conv_b),
        "dt_bias": row(dt_bias),
        "a2": (-jnp.exp(a_log[l].astype(F32)) * LOG2E).reshape(1, SSD_HEADS),
        "dskip_exp": jnp.repeat(d_skip[l], SSD_HEADDIM).reshape(1, D_INNER),
        "norm_w": row(ssm_norm_w),
        "e2": jnp.tile(head_to_lanes, (2, 1)).astype(BF16),
        "e3": jnp.tile(head_to_lanes, (3, 1)).astype(BF16),
        "tri": jnp.tile(jnp.tril(jnp.ones((CHUNK, CHUNK), F32)), (1, 3)).astype(BF16),
        "u": {n: _suffix_sum_matrix(n) for n in (64, 256)},
        "w_br_ssd": w_br_ssd[l].astype(BF16),
        "w_br_attn": w_br_attn[l].astype(BF16),
        "w_out": w_out[l].astype(BF16),
        "ln1_g": row(ln1_g),
        "ln1_b": row(ln1_b),
        "w_up": w_up[l].astype(BF16),
        "b_up": row(b_up),
        "w_down": w_down[l].astype(BF16),
        "b_down": row(b_down),
        "ln2_g": row(ln2_g),
        "ln2_b": row(ln2_b),
    }


def kernel(x_prompt, x_sample, cache_conv, state_ssm, cache_k, cache_v, w_in, b_gate, conv_w, conv_b, dt_bias, a_log, d_skip, ssm_norm_w, w_br_ssd, w_br_attn, w_out, ln1_g, ln1_b, w_up, b_up, w_down, b_down, ln2_g, ln2_b):
    hp, hs = x_prompt, x_sample
    outs_p, outs_s = [], []
    for l in range(DEPTH):
        p = _layer_params(l, w_in, b_gate, conv_w, conv_b, dt_bias, a_log, d_skip, ssm_norm_w, w_br_ssd, w_br_attn,
                          w_out, ln1_g, ln1_b, w_up, b_up, w_down, b_down, ln2_g, ln2_b)
        hp, c1, s1, k1, v1 = _trunk(hp, None, None, None, None, p)
        outs_p.append((c1, s1, k1, v1))
        hs, c2, s2, k2, v2 = _trunk(hs, cache_conv[l], state_ssm[l], cache_k[l], cache_v[l], p)
        outs_s.append((c2, s2, k2, v2))
    stack = lambda outs, i: jnp.stack([o[i] for o in outs])
    return (hp, hs, stack(outs_p, 0), stack(outs_p, 1), stack(outs_p, 2), stack(outs_p, 3),
            stack(outs_s, 0), stack(outs_s, 1), stack(outs_s, 2), stack(outs_s, 3))
```

```python
import functools
import math

import jax
import jax.numpy as jnp
from jax import lax
from jax.experimental import pallas as pl
from jax.experimental.pallas import tpu as pltpu

D_MODEL = 2048
D_INNER = 2 * D_MODEL
SSD_HEADDIM = 64
SSD_HEADS = D_INNER // SSD_HEADDIM
SSD_GROUPS = 8
HEADS_PER_GROUP = SSD_HEADS // SSD_GROUPS
GROUP_W = D_INNER // SSD_GROUPS
D_STATE = 128
CONV_W = 4
CONV_DIM = D_INNER + 2 * SSD_GROUPS * D_STATE
SB_HEADDIM = 128
SB_HEADS = D_MODEL // SB_HEADDIM
D_ATTN = SB_HEADS * SB_HEADDIM
SB_SCALE = 1.0 / math.sqrt(SB_HEADDIM)
D_FF = 4 * D_MODEL
DEPTH = 1
ALPHA = (2.0 * DEPTH) ** 0.25
LOG2E = math.log2(math.e)
SP2_CLAMP = 126.0
R_STOP = 160.0
LN_EPS = 1e-5
RMS_EPS = 1e-5
CHUNK = 64
ATTN_HEADS_PER_STEP = 4
ATTN_Q_TILES_PER_STEP = 2
SSD_GROUP_UNROLL = 8
SSD_CHUNKS_PER_STEP = 4
CONV_ROWS = 128
INPROJ_TM, INPROJ_TN = 1024, 1024

V7X_VMEM_BYTES = 64 * 1024 * 1024
VMEM_LIMIT = 56 * 1024 * 1024

F32 = jnp.float32
BF16 = jnp.bfloat16


def _cparams(sem):
    return pltpu.CompilerParams(dimension_semantics=sem, vmem_limit_bytes=VMEM_LIMIT)


def _sigmoid(x):
    return 1.0 / (1.0 + jnp.exp(-x))


def _softplus(x):
    return jnp.maximum(x, 0.0) + jnp.log1p(jnp.exp(-jnp.abs(x)))


def _split_bf16(x):
    hi = x.astype(BF16)
    lo = (x - hi.astype(F32)).astype(BF16)
    return hi, lo


def _split3_bf16(x):
    hi = x.astype(BF16)
    r1 = x - hi.astype(F32)
    mid = r1.astype(BF16)
    lo = (r1 - mid.astype(F32)).astype(BF16)
    return hi, mid, lo


def _layer_norm(x, g, b):
    mu = jnp.mean(x, axis=-1, keepdims=True)
    xc = x - mu
    var = jnp.mean(xc * xc, axis=-1, keepdims=True)
    return xc * lax.rsqrt(var + LN_EPS) * g + b


P1_XBC, P1_ZS, P1_GATE = 0, CONV_DIM, CONV_DIM + D_INNER
P1_W = P1_GATE + 2 * D_MODEL
P2_Q, P2_K, P2_V, P2_W = 0, D_ATTN, 2 * D_ATTN, 3 * D_ATTN


def _inproj_kernel(*refs, ends, seq_tiles):
    if seq_tiles is None:
        x_ref, wa_ref, wb_ref, wdt_ref, bg_ref, p1_ref, k_ref, v_ref, p2_ref, dt_ref = refs
    else:
        (x_ref, wa_ref, wb_ref, wdt_ref, bg_ref, cw_ref, cb_ref,
         p1_ref, k_ref, v_ref, p2_ref, dt_ref, tail_ref, hist) = refs
    i, j = pl.program_id(0), pl.program_id(1)
    ez, exbc, eq, ek, ev, _ = ends
    tm = x_ref.shape[0]

    def proj(w_ref):
        return jnp.dot(x_ref[...], w_ref[...], preferred_element_type=F32)

    @pl.when(j == 0)
    def _():
        dt_ref[...] = jnp.dot(x_ref[...], wdt_ref[...], preferred_element_type=F32)

    @pl.when(j < ez)
    def _():
        acc = proj(wa_ref)
        p1_ref[...] = acc * _sigmoid(acc)

    @pl.when((j >= ez) & (j < exbc))
    def _():
        if seq_tiles is None:
            p1_ref[...] = proj(wa_ref)
        else:
            jx = j - ez
            prev = jnp.where(i % seq_tiles == 0, 0.0, hist[jx])
            w = wa_ref[...]
            for r0 in range(0, tm, CONV_ROWS):
                raw = jnp.dot(x_ref[r0:r0 + CONV_ROWS, :], w, preferred_element_type=F32)
                ext = jnp.concatenate([prev, raw], axis=0)
                acc = cb_ref[...] + cw_ref[CONV_W - 1:CONV_W, :] * raw
                for t in range(CONV_W - 1):
                    acc = acc + cw_ref[t:t + 1, :] * ext[5 + t:5 + t + CONV_ROWS, :]
                p1_ref[r0:r0 + CONV_ROWS, :] = acc * _sigmoid(acc)
                prev = raw[CONV_ROWS - 8:, :]
            hist[jx] = prev
            tail_ref[0] = prev

    @pl.when((j >= exbc) & (j < eq))
    def _():
        p2_ref[...] = (proj(wb_ref) * (SB_SCALE * LOG2E)).astype(BF16)

    @pl.when((j >= eq) & (j < ek))
    def _():
        acc = proj(wb_ref)
        k_ref[...] = acc
        p2_ref[...] = acc.astype(BF16)

    @pl.when((j >= ek) & (j < ev))
    def _():
        acc = proj(wb_ref)
        v_ref[...] = acc
        p2_ref[...] = acc.astype(BF16)

    @pl.when(j >= ev)
    def _():
        p1_ref[...] = _sigmoid(proj(wb_ref) + bg_ref[...])


def _in_proj(xb, w_a, w_b, w_dt, b_gate, conv_w, conv_b, tm, tn, seq_tiles):
    m = xb.shape[0]
    n_i = m // tm
    widths = (D_INNER, CONV_DIM, D_ATTN, D_ATTN, D_ATTN, 2 * D_MODEL)
    ends, e = [], 0
    for w in widths:
        assert w % tn == 0
        e += w // tn
        ends.append(e)
    ez, exbc, eq, ek, ev, eg = ends

    def parked(lo, hi):
        return lambda i, j: (i, jnp.clip(j - lo, 0, hi - lo - 1))

    def p1_map(i, j):
        blk = jnp.where(j < ez, j + P1_ZS // tn,
                        jnp.where(j < exbc, j - ez + P1_XBC // tn,
                                  jnp.where(j < ev, (exbc - 1 - ez) + P1_XBC // tn, j - ev + P1_GATE // tn)))
        return (i, blk)

    xbc_blk = lambda i, j: (0, jnp.clip(j - ez, 0, exbc - ez - 1))
    in_specs = [
        pl.BlockSpec((tm, D_MODEL), lambda i, j: (i, 0), pipeline_mode=pl.Buffered(1)),
        pl.BlockSpec((D_MODEL, tn), lambda i, j: (0, jnp.minimum(j, exbc - 1))),
        pl.BlockSpec((D_MODEL, tn), lambda i, j: (0, jnp.maximum(j - exbc, 0))),
        pl.BlockSpec((D_MODEL, 128), lambda i, j: (0, 0)),
        pl.BlockSpec((1, tn), lambda i, j: (0, jnp.clip(j - ev, 0, eg - ev - 1))),
    ]
    args = [xb, w_a, w_b, w_dt, b_gate]
    out_shapes = [
        jax.ShapeDtypeStruct((m, P1_W), F32),
        jax.ShapeDtypeStruct((m, D_ATTN), F32),
        jax.ShapeDtypeStruct((m, D_ATTN), F32),
        jax.ShapeDtypeStruct((m, P2_W), BF16),
        jax.ShapeDtypeStruct((m, 128), F32),
    ]
    out_specs = [
        pl.BlockSpec((tm, tn), p1_map),
        pl.BlockSpec((tm, tn), parked(eq, ek)),
        pl.BlockSpec((tm, tn), parked(ek, ev)),
        pl.BlockSpec((tm, tn), parked(exbc, ev)),
        pl.BlockSpec((tm, 128), lambda i, j: (i, 0)),
    ]
    scratch = []
    if seq_tiles is not None:
        in_specs += [pl.BlockSpec((CONV_W, tn), xbc_blk), pl.BlockSpec((1, tn), xbc_blk)]
        args += [conv_w, conv_b]
        out_shapes.append(jax.ShapeDtypeStruct((n_i, 8, CONV_DIM), F32))
        out_specs.append(pl.BlockSpec((1, 8, tn), lambda i, j: (i, 0, jnp.clip(j - ez, 0, exbc - ez - 1))))
        scratch.append(pltpu.VMEM((exbc - ez, 8, tn), F32))
    return pl.pallas_call(
        functools.partial(_inproj_kernel, ends=tuple(ends), seq_tiles=seq_tiles),
        grid=(n_i, eg),
        in_specs=in_specs,
        out_specs=out_specs,
        out_shape=out_shapes,
        scratch_shapes=scratch,
        compiler_params=_cparams(("arbitrary", "arbitrary")),
        name="in_proj",
    )(*args)


def _ssd_kernel(*refs, has_history, n_sub):
    if has_history:
        (xbc_ref, dtr_ref, cprev_ref, h0_ref, convw_ref, convb_ref, dtb_ref, a2_ref, dskip_ref,
         nw_ref, e2_ref, e3_ref, tri_ref, y_ref, ht_ref, xpad, state) = refs
    else:
        (xbc_ref, dtr_ref, dtb_ref, a2_ref, dskip_ref,
         nw_ref, e2_ref, e3_ref, tri_ref, y_ref, ht_ref, state) = refs
    c = pl.program_id(1)
    nc = pl.num_programs(1)
    L = CHUNK

    @pl.when(c == 0)
    def _():
        if has_history:
            xpad[0:8, :] = cprev_ref[0]
            state[...] = h0_ref[0]
        else:
            state[...] = jnp.zeros(state.shape, F32)

    tri3 = tri_ref[...]
    row = lax.broadcasted_iota(jnp.int32, (L, GROUP_W), 0)
    lane = lax.broadcasted_iota(jnp.int32, (L, GROUP_W), 1)
    pos_in_head = lane & (SSD_HEADDIM - 1)
    diag_sel = row == pos_in_head
    causal = row >= pos_in_head
    lane128 = lax.broadcasted_iota(jnp.int32, (L, 128), 1)
    lo_half = lane128 < SSD_HEADDIM
    zero = jnp.zeros((L, 128), BF16)

    def chunk_body(ci, carry):
        rows = pl.ds(pl.multiple_of(ci * L, L), L)
        if has_history:
            xpad[8:8 + L, :] = xbc_ref[0, rows, 0:CONV_DIM]

        def conv_cols(start, width):
            sl = pl.ds(start, width)
            if not has_history:
                return xbc_ref[0, rows, sl]
            acc = convb_ref[:, sl] + convw_ref[CONV_W - 1:CONV_W, sl] * xpad[8:8 + L, sl]
            for t in range(CONV_W - 1):
                acc = acc + convw_ref[t:t + 1, sl] * xpad[5 + t:5 + t + L, sl]
            return acc * _sigmoid(acc)

        dt = _softplus(dtr_ref[0, rows, :][:, :SSD_HEADS] + dtb_ref[...])
        acs_c = jnp.dot(tri3, jnp.concatenate(_split3_bf16(dt * a2_ref[...]), axis=0),
                        preferred_element_type=F32)
        decay_in = jnp.exp2(acs_c)
        decay_out = jnp.exp2(acs_c[L - 1:L, :] - acs_c)
        lhs2 = jnp.concatenate([jnp.concatenate(_split_bf16(v), axis=1) for v in (dt, decay_in, decay_out)],
                               axis=0)
        lhs3 = jnp.concatenate(_split3_bf16(acs_c), axis=1)
        groups_body = functools.partial(_ssd_groups, refs=(xbc_ref, dskip_ref, nw_ref, e2_ref, e3_ref, y_ref, state),
                                        conv_cols=conv_cols, rows=rows, lhs2=lhs2, lhs3=lhs3,
                                        masks=(diag_sel, causal, lo_half, zero))
        lax.fori_loop(0, SSD_GROUPS // SSD_GROUP_UNROLL, groups_body, 0)
        if has_history:
            xpad[0:8, :] = xpad[L:L + 8, :]
        return carry

    lax.fori_loop(0, n_sub, chunk_body, 0)

    @pl.when(c == nc - 1)
    def _():
        ht_ref[0] = state[...]


def _ssd_groups(it, carry, *, refs, conv_cols, rows, lhs2, lhs3, masks):
    xz_ref, dskip_ref, nw_ref, e2_ref, e3_ref, y_ref, state = refs
    diag_sel, causal, lo_half, zero = masks
    L = CHUNK
    gs = [it * SSD_GROUP_UNROLL + k for k in range(SSD_GROUP_UNROLL)]
    xoffs = [pl.multiple_of(g * GROUP_W, GROUP_W) for g in gs]
    xss, bgs, cgs = [], [], []
    for g, xoff in zip(gs, xoffs):
        boff = pl.multiple_of(D_INNER + g * D_STATE, D_STATE)
        coff = pl.multiple_of(D_INNER + SSD_GROUPS * D_STATE + g * D_STATE, D_STATE)
        xss.append(conv_cols(xoff, GROUP_W))
        bgs.append(conv_cols(boff, D_STATE).astype(BF16))
        cgs.append(conv_cols(coff, D_STATE).astype(BF16))

    spreads = [jnp.dot(lhs2, e2_ref[:, pl.ds(xoff, GROUP_W)], preferred_element_type=F32) for xoff in xoffs]
    acss = [jnp.dot(lhs3, e3_ref[:, pl.ds(xoff, GROUP_W)], preferred_element_type=F32) for xoff in xoffs]
    cb2s = [lax.dot_general(cg, jnp.concatenate([bg, bg], axis=0), (((1,), (1,)), ((), ())),
                            preferred_element_type=F32) for cg, bg in zip(cgs, bgs)]

    decay_ins, xdds, mmats, bds = [], [], [], []
    for xs, spread, acs, cb2 in zip(xss, spreads, acss, cb2s):
        decay_ins.append(spread[L:2 * L])
        xd = xs * spread[0:L]
        xdds.append((xd * spread[2 * L:3 * L]).astype(BF16))
        xdb = xd.astype(BF16)
        rowform = jnp.sum(jnp.where(diag_sel, acs, 0.0), axis=0, keepdims=True)
        lmat = jnp.where(causal, jnp.exp2(acs - rowform), 0.0)
        mmats.append((lmat * jnp.concatenate([cb2] * (GROUP_W // 128), axis=1)).astype(BF16))
        bds.append([jnp.concatenate([jnp.where(lo_half, xdb[:, 128 * p:128 * (p + 1)], zero),
                                     jnp.where(lo_half, zero, xdb[:, 128 * p:128 * (p + 1)])], axis=0)
                    for p in range(GROUP_W // 128)])

    sts = [state[g] for g in gs]
    ydiags = [jnp.concatenate([jnp.dot(mmat[:, 128 * p:128 * (p + 1)], bd[p], preferred_element_type=F32)
                               for p in range(GROUP_W // 128)], axis=1) for mmat, bd in zip(mmats, bds)]
    yoffs = [jnp.dot(cg, st.astype(BF16), preferred_element_type=F32) for cg, st in zip(cgs, sts)]
    snews = [lax.dot_general(bg, xdd, (((0,), (0,)), ((), ())), preferred_element_type=F32)
             for bg, xdd in zip(bgs, xdds)]

    for k, (g, xoff) in enumerate(zip(gs, xoffs)):
        cols = pl.ds(xoff, GROUP_W)
        y = ydiags[k] + yoffs[k] * decay_ins[k] + dskip_ref[:, cols] * xss[k]
        state[g] = sts[k] * decay_ins[k][L - 1:L, :] + snews[k]
        yg = y * xz_ref[0, rows, pl.ds(CONV_DIM + xoff, GROUP_W)]
        ms = jnp.mean(yg * yg, axis=-1, keepdims=True)
        y_ref[0, rows, cols] = (yg * lax.rsqrt(ms + RMS_EPS) * nw_ref[:, cols]).astype(y_ref.dtype)
    return carry


def _ssd(p1, dtr, conv_prev8, h0t, conv_w, conv_b, dt_bias, a2, dskip_exp, norm_w, e2, e3, tri, b, t):
    has_history = conv_prev8 is not None
    n_sub = min(SSD_CHUNKS_PER_STEP, t // CHUNK)
    rows = n_sub * CHUNK
    assert t % rows == 0
    tok = lambda bi, c: (bi, c, 0)
    const2 = lambda bi, c: (0, 0)
    p1 = p1.reshape(b, t, P1_W)
    in_specs = [
        pl.BlockSpec((1, rows, CONV_DIM + D_INNER), tok),
        pl.BlockSpec((1, rows, 128), tok),
    ]
    args = [p1, dtr.reshape(b, t, 128)]
    assert (P1_XBC, P1_ZS) == (0, CONV_DIM)
    if has_history:
        in_specs += [
            pl.BlockSpec((1, 8, CONV_DIM), lambda bi, c: (bi, 0, 0)),
            pl.BlockSpec((1, SSD_GROUPS, D_STATE, GROUP_W), lambda bi, c: (bi, 0, 0, 0)),
            pl.BlockSpec((CONV_W, CONV_DIM), const2),
            pl.BlockSpec((1, CONV_DIM), const2),
        ]
        args += [conv_prev8, h0t, conv_w, conv_b]
    in_specs += [
        pl.BlockSpec((1, SSD_HEADS), const2),
        pl.BlockSpec((1, SSD_HEADS), const2),
        pl.BlockSpec((1, D_INNER), const2),
        pl.BlockSpec((1, D_INNER), const2),
        pl.BlockSpec((2 * SSD_HEADS, D_INNER), const2),
        pl.BlockSpec((3 * SSD_HEADS, D_INNER), const2),
        pl.BlockSpec((CHUNK, 3 * CHUNK), const2),
    ]
    args += [dt_bias, a2, dskip_exp, norm_w, e2, e3, tri]
    scratch = [pltpu.VMEM((SSD_GROUPS, D_STATE, GROUP_W), F32)]
    if has_history:
        scratch.insert(0, pltpu.VMEM((8 + CHUNK, CONV_DIM), F32))
    return pl.pallas_call(
        functools.partial(_ssd_kernel, has_history=has_history, n_sub=n_sub),
        grid=(b, t // rows),
        in_specs=in_specs,
        out_specs=[
            pl.BlockSpec((1, rows, D_INNER), tok),
            pl.BlockSpec((1, SSD_GROUPS, D_STATE, GROUP_W), lambda bi, c: (bi, 0, 0, 0)),
        ],
        out_shape=(
            jax.ShapeDtypeStruct((b, t, D_INNER), BF16),
            jax.ShapeDtypeStruct((b, SSD_GROUPS, D_STATE, GROUP_W), F32),
        ),
        scratch_shapes=scratch,
        compiler_params=_cparams(("parallel", "arbitrary")),
        name="conv_ssd",
    )(*args)


def _sb_tile_parts(qs, kbs, vbs, u, visibles):
    n = len(qs)
    ts = [lax.dot_general(qs[h], kbs[h], (((1,), (1,)), ((), ())), preferred_element_type=F32) for h in range(n)]
    sp2s, hilos = [], []
    for t, visible in zip(ts, visibles):
        sp2 = jnp.maximum(t, jnp.log(1.0 + jnp.exp2(jnp.minimum(t, SP2_CLAMP))) * LOG2E)
        if visible is not None:
            sp2 = jnp.where(visible, sp2, 0.0)
        sp2s.append(sp2)
        hilos.append(jnp.concatenate(_split_bf16(sp2), axis=1))
    sums = [jnp.dot(hl, u, preferred_element_type=F32) for hl in hilos]
    ws = []
    for t, s, visible in zip(ts, sums, visibles):
        w = jnp.exp2(t - s)
        if visible is not None:
            w = jnp.where(visible, w, 0.0)
        ws.append(w.astype(BF16))
    pvs = [jnp.dot(ws[h], vbs[h], preferred_element_type=F32) for h in range(n)]
    return [jnp.sum(sp2, axis=-1, keepdims=True) for sp2 in sp2s], pvs


def _causal_mask(tq, tk):
    return lax.broadcasted_iota(jnp.int32, (tq, tk), 1) < lax.broadcasted_iota(jnp.int32, (tq, tk), 0)


def _sb_tiles(qs, kbs, vbs, u, carry, masked):
    visible = _causal_mask(qs[0].shape[0], kbs[0].shape[0]) if masked else None
    rowsums, pvs = _sb_tile_parts(qs, kbs, vbs, u, [visible] * len(qs))
    return tuple((r + rs, acc + jnp.exp2(-r) * pv) for (r, acc), rs, pv in zip(carry, rowsums, pvs))


def _visit_key_blocks(step, n_blocks, carry):
    def r_min(carry):
        return functools.reduce(jnp.minimum, [jnp.min(r) for r, _ in carry])

    def cond(state):
        i, rmin, _ = state
        return jnp.logical_and(i < n_blocks, rmin < R_STOP)

    def body(state):
        i, _, carry = state
        carry = step(i, carry)
        return i + 1, r_min(carry), carry

    return lax.while_loop(cond, body, (jnp.int32(0), r_min(carry), carry))[2]


def _head_cols(hh):
    return slice(hh * SB_HEADDIM, (hh + 1) * SB_HEADDIM)


def _attn_prompt_kernel(q_ref, k_ref, v_ref, u_ref, o_ref, *, tq, hp, n_q):
    u = u_ref[...]
    for qq in range(n_q):
        qi = pl.program_id(2) * n_q + qq
        rows = slice(qq * tq, (qq + 1) * tq)
        qs = [q_ref[0, rows, _head_cols(hh)] for hh in range(hp)]

        def kv(jb):
            off = pl.multiple_of(jb * tq, tq)
            return ([k_ref[0, pl.ds(off, tq), _head_cols(hh)] for hh in range(hp)],
                    [v_ref[0, pl.ds(off, tq), _head_cols(hh)] for hh in range(hp)])

        kd, vd = kv(qi)
        kp, vp = kv(jnp.maximum(qi - 1, 0))
        rows_visible = jnp.where(qi >= 1, tq, 0)
        prev_visible = lax.broadcasted_iota(jnp.int32, (tq, tq), 0) < rows_visible
        rowsums, pvs = _sb_tile_parts(qs + qs, kd + kp, vd + vp, u,
                                      [_causal_mask(tq, tq)] * hp + [prev_visible] * hp)
        carry = tuple((rowsums[hh] + rowsums[hp + hh], pvs[hh] + jnp.exp2(-rowsums[hh]) * pvs[hp + hh])
                      for hh in range(hp))

        def step(i, c, qi=qi, qs=qs, kv=kv):
            kbs, vbs = kv(qi - 2 - i)
            return _sb_tiles(qs, kbs, vbs, u, c, False)

        carry = _visit_key_blocks(step, jnp.maximum(qi - 1, 0), carry)
        for hh in range(hp):
            o_ref[0, rows, _head_cols(hh)] = carry[hh][1].astype(o_ref.dtype)


def _attn_prompt(p2, u, b, t, tq, hp):
    q3 = k3 = v3 = p2.reshape(b, t, P2_W)
    w = hp * SB_HEADDIM
    n_q = ATTN_Q_TILES_PER_STEP
    return pl.pallas_call(
        functools.partial(_attn_prompt_kernel, tq=tq, hp=hp, n_q=n_q),
        grid=(b, SB_HEADS // hp, t // (tq * n_q)),
        in_specs=[
            pl.BlockSpec((1, tq * n_q, w), lambda bi, h, qi: (bi, qi, P2_Q // w + h)),
            pl.BlockSpec((1, t, w), lambda bi, h, qi: (bi, 0, P2_K // w + h)),
            pl.BlockSpec((1, t, w), lambda bi, h, qi: (bi, 0, P2_V // w + h)),
            pl.BlockSpec((2 * tq, tq), lambda bi, h, qi: (0, 0)),
        ],
        out_specs=pl.BlockSpec((1, tq * n_q, w), lambda bi, h, qi: (bi, qi, h)),
        out_shape=jax.ShapeDtypeStruct((b, t, D_ATTN), BF16),
        compiler_params=_cparams(("parallel", "parallel", "arbitrary")),
        name="sb_attn_prompt",
    )(q3, k3, v3, u)


def _attn_sample_kernel(q_ref, kn_ref, vn_ref, kc_ref, vc_ref, un_ref, uc_ref, o_ref, *, tn, tc, n_cache, hp, hb):
    un, uc = un_ref[...], uc_ref[...]
    for h0 in range(0, hb, hp):
        heads = range(h0, h0 + hp)
        qs = [q_ref[0, :, _head_cols(hh)] for hh in heads]
        init = tuple((jnp.zeros((tn, 1), F32), jnp.zeros((tn, SB_HEADDIM), F32)) for _ in heads)
        carry = _sb_tiles(qs, [kn_ref[0, :, _head_cols(hh)] for hh in heads],
                          [vn_ref[0, :, _head_cols(hh)] for hh in heads], un, init, True)

        def body(i, carry, heads=heads, qs=qs):
            off = pl.multiple_of((n_cache - 1 - i) * tc, tc)
            kbs = [kc_ref[0, pl.ds(off, tc), hh, :].astype(BF16) for hh in heads]
            vbs = [vc_ref[0, pl.ds(off, tc), hh, :].astype(BF16) for hh in heads]
            return _sb_tiles(qs, kbs, vbs, uc, carry, False)

        carry = _visit_key_blocks(body, n_cache, carry)
        for hh, (_, acc) in zip(heads, carry):
            o_ref[0, :, _head_cols(hh)] = acc.astype(o_ref.dtype)


def _attn_sample(p2, cache_k, cache_v, un, uc, b, t, past, tc, hp):
    q3 = k3 = v3 = p2.reshape(b, t, P2_W)
    hb = 8
    w = hb * SB_HEADDIM
    new = lambda bi, h: (bi, 0, h)
    cache = lambda bi, h: (bi, 0, h, 0)
    return pl.pallas_call(
        functools.partial(_attn_sample_kernel, tn=t, tc=tc, n_cache=past // tc, hp=hp, hb=hb),
        grid=(b, SB_HEADS // hb),
        in_specs=[
            pl.BlockSpec((1, t, w), lambda bi, h: (bi, 0, P2_Q // w + h)),
            pl.BlockSpec((1, t, w), lambda bi, h: (bi, 0, P2_K // w + h)),
            pl.BlockSpec((1, t, w), lambda bi, h: (bi, 0, P2_V // w + h)),
            pl.BlockSpec((1, past, hb, SB_HEADDIM), cache),
            pl.BlockSpec((1, past, hb, SB_HEADDIM), cache),
            pl.BlockSpec((2 * t, t), lambda bi, h: (0, 0)),
            pl.BlockSpec((2 * tc, tc), lambda bi, h: (0, 0)),
        ],
        out_specs=pl.BlockSpec((1, t, w), new),
        out_shape=jax.ShapeDtypeStruct((b, t, D_ATTN), BF16),
        compiler_params=_cparams(("parallel", "parallel")),
        name="sb_attn_sample",
    )(q3, k3, v3, cache_k, cache_v, un, uc)


def _merge_kernel(ya_ref, yb_ref, wa_ref, wb_ref, ga_ref, gb_ref, o_ref):
    a = jnp.dot(ya_ref[...], wa_ref[...], preferred_element_type=F32)
    bb = jnp.dot(yb_ref[...], wb_ref[...], preferred_element_type=F32)
    o_ref[...] = (ga_ref[...] * a + gb_ref[...] * bb).astype(o_ref.dtype)


def _merge(ya, yb, wa, wb, p1, tm, tn):
    m = ya.shape[0]
    nb = D_MODEL // tn
    g0 = P1_GATE // tn
    return pl.pallas_call(
        _merge_kernel,
        grid=(m // tm, nb),
        in_specs=[
            pl.BlockSpec((tm, D_INNER), lambda i, j: (i, 0)),
            pl.BlockSpec((tm, D_ATTN), lambda i, j: (i, 0)),
            pl.BlockSpec((D_INNER, tn), lambda i, j: (0, j)),
            pl.BlockSpec((D_ATTN, tn), lambda i, j: (0, j)),
            pl.BlockSpec((tm, tn), lambda i, j: (i, g0 + j)),
            pl.BlockSpec((tm, tn), lambda i, j: (i, g0 + nb + j)),
        ],
        out_specs=pl.BlockSpec((tm, tn), lambda i, j: (i, j)),
        out_shape=jax.ShapeDtypeStruct((m, D_MODEL), BF16),
        compiler_params=_cparams(("parallel", "arbitrary")),
        name="branch_merge",
    )(ya, yb, wa, wb, p1, p1)


def _outproj_kernel(m_ref, w_ref, x_ref, g_ref, b_ref, o_ref):
    mix = jnp.dot(m_ref[...], w_ref[...], preferred_element_type=F32)
    o_ref[...] = _layer_norm(ALPHA * x_ref[...] + mix, g_ref[...], b_ref[...])


def _outproj_ln(merged, w_out, x, g, bb, tm):
    m = x.shape[0]
    row = lambda i: (i, 0)
    const = lambda i: (0, 0)
    return pl.pallas_call(
        _outproj_kernel,
        grid=(m // tm,),
        in_specs=[
            pl.BlockSpec((tm, D_MODEL), row),
            pl.BlockSpec((D_MODEL, D_MODEL), const),
            pl.BlockSpec((tm, D_MODEL), row),
            pl.BlockSpec((1, D_MODEL), const),
            pl.BlockSpec((1, D_MODEL), const),
        ],
        out_specs=pl.BlockSpec((tm, D_MODEL), row),
        out_shape=jax.ShapeDtypeStruct((m, D_MODEL), F32),
        compiler_params=_cparams(("parallel",)),
        name="outproj_ln",
    )(merged, w_out, x, g, bb)


def _mlp_kernel(x_ref, wu_ref, bu_ref, wd_ref, bd_ref, g_ref, b_ref, o_ref, xb, acc):
    f = pl.program_id(1)

    @pl.when(f == 0)
    def _():
        xb[...] = x_ref[...].astype(BF16)
        acc[...] = jnp.zeros(acc.shape, F32)

    h = jnp.dot(xb[...], wu_ref[...], preferred_element_type=F32) + bu_ref[...]
    h = jnp.square(jnp.maximum(h, 0.0)).astype(BF16)
    acc[...] += jnp.dot(h, wd_ref[...], preferred_element_type=F32)

    @pl.when(f == pl.num_programs(1) - 1)
    def _():
        o_ref[...] = _layer_norm(ALPHA * x_ref[...] + acc[...] + bd_ref[...], g_ref[...], b_ref[...])


def _mlp_ln(x1, w_up, b_up, w_down, b_down, g, bb, tm, tf):
    m = x1.shape[0]
    row = lambda i, f: (i, 0)
    const = lambda i, f: (0, 0)
    return pl.pallas_call(
        _mlp_kernel,
        grid=(m // tm, D_FF // tf),
        in_specs=[
            pl.BlockSpec((tm, D_MODEL), row),
            pl.BlockSpec((D_MODEL, tf), lambda i, f: (0, f)),
            pl.BlockSpec((1, tf), lambda i, f: (0, f)),
            pl.BlockSpec((tf, D_MODEL), lambda i, f: (f, 0)),
            pl.BlockSpec((1, D_MODEL), const),
            pl.BlockSpec((1, D_MODEL), const),
            pl.BlockSpec((1, D_MODEL), const),
        ],
        out_specs=pl.BlockSpec((tm, D_MODEL), row),
        out_shape=jax.ShapeDtypeStruct((m, D_MODEL), F32),
        scratch_shapes=[pltpu.VMEM((tm, D_MODEL), BF16), pltpu.VMEM((tm, D_MODEL), F32)],
        compiler_params=_cparams(("parallel", "arbitrary")),
        name="mlp_ln",
    )(x1, w_up, b_up, w_down, b_down, g, bb)


def _trunk(x, conv_prev, h0, k_past, v_past, p):
    b, t, _ = x.shape
    m = b * t
    x2 = x.reshape(m, D_MODEL)
    tm = min(m, INPROJ_TM)
    keep = CONV_W - 1
    if conv_prev is None:
        assert t % tm == 0
        p1, k, v, p2, dtr, tails = _in_proj(x2.astype(BF16), p["w_a"], p["w_b"], p["w_dt"], p["b_gate"],
                                            p["conv_w"], p["conv_b"], tm, INPROJ_TN, t // tm)
        conv_prev8, h0t = None, None
        conv_new = tails.reshape(b, t // tm, 8, CONV_DIM)[:, -1, 8 - keep:, :]
    else:
        p1, k, v, p2, dtr = _in_proj(x2.astype(BF16), p["w_a"], p["w_b"], p["w_dt"], p["b_gate"],
                                     None, None, tm, INPROJ_TN, None)
        conv_prev8 = jnp.pad(conv_prev, ((0, 0), (8 - keep, 0), (0, 0)))
        h0t = h0.reshape(b, SSD_GROUPS, GROUP_W, D_STATE).transpose(0, 1, 3, 2)
        conv_new = p1.reshape(b, t, P1_W)[:, t - keep:, P1_XBC:P1_XBC + CONV_DIM]
    y_ssd, ht = _ssd(p1, dtr, conv_prev8, h0t, p["conv_w"], p["conv_b"], p["dt_bias"], p["a2"],
                     p["dskip_exp"], p["norm_w"], p["e2"], p["e3"], p["tri"], b, t)
    h_new = ht.transpose(0, 1, 3, 2).reshape(b, SSD_HEADS, SSD_HEADDIM, D_STATE)

    if k_past is None:
        tq = 256
        y_sb = _attn_prompt(p2, p["u"][tq], b, t, tq, ATTN_HEADS_PER_STEP)
    else:
        tc = 256
        y_sb = _attn_sample(p2, k_past, v_past, p["u"][t], p["u"][tc], b, t, k_past.shape[1], tc,
                            ATTN_HEADS_PER_STEP)

    merged = _merge(y_ssd.reshape(m, D_INNER), y_sb.reshape(m, D_ATTN), p["w_br_ssd"], p["w_br_attn"], p1,
                    min(m, 1024), 512)
    x1 = _outproj_ln(merged, p["w_out"], x2, p["ln1_g"], p["ln1_b"], min(m, 512))
    x_out = _mlp_ln(x1, p["w_up"], p["b_up"], p["w_down"], p["b_down"], p["ln2_g"], p["ln2_b"], min(m, 512), 1024)
    kv_shape = (b, t, SB_HEADS, SB_HEADDIM)
    return x_out.reshape(b, t, D_MODEL), conv_new, h_new, k.reshape(kv_shape), v.reshape(kv_shape)


def _suffix_sum_matrix(n):
    u = jnp.tril(jnp.ones((n, n), F32)).astype(BF16)
    return jnp.concatenate([u, u], axis=0)


def _layer_params(l, w_in, b_gate, conv_w, conv_b, dt_bias, a_log, d_skip, ssm_norm_w, w_br_ssd, w_br_attn, w_out,
                  ln1_g, ln1_b, w_up, b_up, w_down, b_down, ln2_g, ln2_b):
    w = w_in[l]
    o_dt = D_INNER + CONV_DIM
    o_q = o_dt + SSD_HEADS
    row = lambda a: a[l].reshape(1, -1)
    head_to_lanes = jnp.arange(SSD_HEADS)[:, None] == (jnp.arange(D_INNER) // SSD_HEADDIM)[None, :]
    return {
        "w_a": w[:, :o_dt].astype(BF16),
        "w_b": w[:, o_q:].astype(BF16),
        "w_dt": jnp.pad(w[:, o_dt:o_q], ((0, 0), (0, 128 - SSD_HEADS))).astype(BF16),
        "b_gate": row(b_gate),
        "conv_w": conv_w[l],
        "conv_b": row(conv_b),
        "dt_bias": row(dt_bias),
        "a2": (-jnp.exp(a_log[l].astype(F32)) * LOG2E).reshape(1, SSD_HEADS),
        "dskip_exp": jnp.repeat(d_skip[l], SSD_HEADDIM).reshape(1, D_INNER),
        "norm_w": row(ssm_norm_w),
        "e2": jnp.tile(head_to_lanes, (2, 1)).astype(BF16),
        "e3": jnp.tile(head_to_lanes, (3, 1)).astype(BF16),
        "tri": jnp.tile(jnp.tril(jnp.ones((CHUNK, CHUNK), F32)), (1, 3)).astype(BF16),
        "u": {n: _suffix_sum_matrix(n) for n in (64, 256)},
        "w_br_ssd": w_br_ssd[l].astype(BF16),
        "w_br_attn": w_br_attn[l].astype(BF16),
        "w_out": w_out[l].astype(BF16),
        "ln1_g": row(ln1_g),
        "ln1_b": row(ln1_b),
        "w_up": w_up[l].astype(BF16),
        "b_up": row(b_up),
        "w_down": w_down[l].astype(BF16),
        "b_down": row(b_down),
        "ln2_g": row(ln2_g),
        "ln2_b": row(ln2_b),
    }


def kernel(x_prompt, x_sample, cache_conv, state_ssm, cache_k, cache_v, w_in, b_gate, conv_w, conv_b, dt_bias, a_log, d_skip, ssm_norm_w, w_br_ssd, w_br_attn, w_out, ln1_g, ln1_b, w_up, b_up, w_down, b_down, ln2_g, ln2_b):
    hp, hs = x_prompt, x_sample
    outs_p, outs_s = [], []
    for l in range(DEPTH):
        p = _layer_params(l, w_in, b_gate, conv_w, conv_b, dt_bias, a_log, d_skip, ssm_norm_w, w_br_ssd, w_br_attn,
                          w_out, ln1_g, ln1_b, w_up, b_up, w_down, b_down, ln2_g, ln2_b)
        hp, c1, s1, k1, v1 = _trunk(hp, None, None, None, None, p)
        outs_p.append((c1, s1, k1, v1))
        hs, c2, s2, k2, v2 = _trunk(hs, cache_conv[l], state_ssm[l], cache_k[l], cache_v[l], p)
        outs_s.append((c2, s2, k2, v2))
    stack = lambda outs, i: jnp.stack([o[i] for o in outs])
    return (hp, hs, stack(outs_p, 0), stack(outs_p, 1), stack(outs_p, 2), stack(outs_p, 3),
            stack(outs_s, 0), stack(outs_s, 1), stack(outs_s, 2), stack(outs_s, 3))
```
